```python
import math
import jax
import jax.numpy as jnp
from jax import lax
import numpy as np

D_MODEL = 2048
BATCH = 4
SEQ = 4096
DEPTH = 1

HEAD_DIM = 128
DIL_GROUPS = ((128, 1), (512, 4), (2048, 16))
DIL_HEADS_PER_GROUP = 4
N_DIL_HEADS = len(DIL_GROUPS) * DIL_HEADS_PER_GROUP
N_DIFF_HEADS = 4
DIFF_DIM = HEAD_DIM // 2
BRANCH_WIDTH = DIL_HEADS_PER_GROUP * HEAD_DIM
N_BRANCHES = 2
ROPE_THETA = 500000.0
ROPE_FRACTION = 4
N_EXPERTS = 16
EXPERT_FF = 2048
EC_CAPACITY = 2
LN_EPS = 1e-5
DIFF_NORM_EPS = 1e-5
Q_BLOCK = 128
ALPHA = (2 * DEPTH) ** 0.25
BETA = (8 * DEPTH) ** -0.25
NEG_INF = -1e30

A_QKV = N_DIL_HEADS * HEAD_DIM
B_QK = N_DIFF_HEADS * 2 * DIFF_DIM
B_V = N_DIFF_HEADS * HEAD_DIM
GATE_COLS = N_BRANCHES * D_MODEL
IN_COLS = 3 * A_QKV + 2 * B_QK + B_V + GATE_COLS

kernel_name = "dilated_diff_attn_ec_moe_deepnorm_block"


def _normal(key, shape, scale):
    return jax.random.normal(key, shape, jnp.float32) * scale


def layer_norm(x, g, b):
    x32 = x.astype(jnp.float32)
    mu = jnp.mean(x32, axis=-1, keepdims=True)
    var = jnp.mean(jnp.square(x32 - mu), axis=-1, keepdims=True)
    y = (x32 - mu) * lax.rsqrt(var + LN_EPS) * g.astype(jnp.float32) + b.astype(jnp.float32)
    return y.astype(x.dtype)


def rope_partial(x, pos):
    dh = x.shape[-1]
    rot = dh // ROPE_FRACTION
    half = rot // 2
    inv_freq = ROPE_THETA ** (-2.0 * jnp.arange(half, dtype=jnp.float32) / rot)
    ang = pos.astype(jnp.float32)[:, None] * inv_freq[None, :]
    shape = (1, pos.shape[0]) + (1,) * (x.ndim - 3) + (half,)
    cos = jnp.cos(ang).reshape(shape).astype(x.dtype)
    sin = jnp.sin(ang).reshape(shape).astype(x.dtype)
    x1 = x[..., :half]
    x2 = x[..., half:rot]
    return jnp.concatenate([x1 * cos - x2 * sin, x2 * cos + x1 * sin, x[..., rot:]], axis=-1)


def band_attention(q, k, v, half):
    b_, n_, L, dh = q.shape
    qb = half
    nb = -(-L // qb)
    Lp = nb * qb
    qp = jnp.pad(q, ((0, 0), (0, 0), (0, Lp - L), (0, 0)))
    kv_pad = ((0, 0), (0, 0), (half, Lp - L + half), (0, 0))
    kp = jnp.pad(k, kv_pad)
    vp = jnp.pad(v, kv_pad)
    kb = qb + 2 * half
    idx = (jnp.arange(nb) * qb)[:, None] + jnp.arange(kb)[None, :]
    k_blk = kp[:, :, idx]
    v_blk = vp[:, :, idx]
    q_blk = qp.reshape(b_, n_, nb, qb, dh)
    s = jnp.einsum('bniqd,bnikd->bniqk', q_blk, k_blk).astype(jnp.float32) * (dh ** -0.5)
    qpos = (jnp.arange(nb) * qb)[:, None] + jnp.arange(qb)[None, :]
    kpos = (idx - half)[:, None, :]
    valid = (jnp.abs(qpos[:, :, None] - kpos) <= half) & (kpos >= 0) & (kpos < L)
    s = jnp.where(valid, s, NEG_INF)
    m = jnp.max(s, axis=-1, keepdims=True)
    p = jnp.exp(s - m)
    den = jnp.sum(p, axis=-1, keepdims=True)
    o = jnp.einsum('bniqk,bnikd->bniqd', (p / den).astype(v.dtype), v_blk)
    lse = (m + jnp.log(den))[..., 0]
    o = o.reshape(b_, n_, Lp, dh)[:, :, :L]
    lse = lse.reshape(b_, n_, Lp)[:, :, :L]
    return o, lse


def dilated_attention(q, k, v):
    b_, S, _, H, dh = q.shape
    outs, lses = [], []
    for g, (window, r) in enumerate(DIL_GROUPS):
        L = S // r

        def to_strided(t):
            return t.reshape(b_, L, r, H, dh).transpose(0, 3, 2, 1, 4).reshape(b_, H * r, L, dh)

        o, lse = band_attention(to_strided(q[:, :, g]), to_strided(k[:, :, g]),
                                to_strided(v[:, :, g]), window // (2 * r))
        outs.append(o.reshape(b_, H, r, L, dh).transpose(0, 3, 2, 1, 4).reshape(b_, S, H, dh))
        lses.append(lse.reshape(b_, H, r, L).transpose(0, 3, 2, 1).reshape(b_, S, H))
    o = jnp.stack(outs, axis=2)
    w = jax.nn.softmax(jnp.stack(lses, axis=2), axis=2)
    return jnp.sum(o * w[..., None].astype(o.dtype), axis=2)


def diff_attention(q, k, v, lam, norm_w, lambda_init):
    b_, S, H, _, dc = q.shape
    dv = v.shape[-1]
    nb = S // Q_BLOCK
    kt = k.transpose(0, 2, 3, 1, 4)
    vt = v.transpose(0, 2, 1, 3)
    q_blocks = q.transpose(0, 2, 3, 1, 4).reshape(b_, H, 2, nb, Q_BLOCK, dc).transpose(3, 0, 1, 2, 4, 5)
    scale = dc ** -0.5

    def one_block(qb):
        s = jnp.einsum('bhcqd,bhckd->bhcqk', qb, kt).astype(jnp.float32) * scale
        p = jax.nn.softmax(s, axis=-1)
        a = p[:, :, 0] - lam * p[:, :, 1]
        return jnp.einsum('bhqk,bhkd->bhqd', a.astype(vt.dtype), vt)

    o = lax.map(one_block, q_blocks)
    o = o.transpose(1, 0, 3, 2, 4).reshape(b_, S, H, dv)
    o32 = o.astype(jnp.float32)
    o32 = o32 * lax.rsqrt(jnp.mean(jnp.square(o32), axis=-1, keepdims=True) + DIFF_NORM_EPS)
    o32 = o32 * norm_w.astype(jnp.float32) * (1.0 - lambda_init)
    return o32.astype(v.dtype)


def expert_choice_ffn(x, w_router, w_gate, w_up, w_down):
    b_, S, D = x.shape
    cap = EC_CAPACITY * S // N_EXPERTS
    aff = jax.nn.softmax(jnp.einsum('bsd,de->bse', x, w_router).astype(jnp.float32), axis=-1)
    gates, idx = lax.top_k(aff.transpose(0, 2, 1), cap)
    xin = jax.vmap(lambda xb, ib: xb[ib])(x, idx)
    h = jax.nn.silu(jnp.einsum('becd,edf->becf', xin, w_gate)) * jnp.einsum('becd,edf->becf', xin, w_up)
    out = jnp.einsum('becf,efd->becd', h, w_down) * gates[..., None].astype(x.dtype)
    y = jax.vmap(lambda ib, ob: jnp.zeros((S, D), x.dtype).at[ib.reshape(-1)].add(ob.reshape(-1, D)))(idx, out)
    return y


def setup_inputs(seed: int = 0) -> dict:
    key = jax.random.key(seed)
    ks = jax.random.split(key, 20)
    d_sc = D_MODEL ** -0.5
    x = _normal(ks[0], (BATCH, SEQ, D_MODEL), 1.0)
    w_in = jnp.concatenate([
        _normal(ks[1], (DEPTH, D_MODEL, 2 * A_QKV), d_sc),
        _normal(ks[2], (DEPTH, D_MODEL, A_QKV), d_sc * BETA),
        _normal(ks[3], (DEPTH, D_MODEL, 2 * B_QK), d_sc),
        _normal(ks[4], (DEPTH, D_MODEL, B_V), d_sc * BETA),
        _normal(ks[5], (DEPTH, D_MODEL, GATE_COLS), d_sc),
    ], axis=-1)
    lambda_q1 = _normal(ks[6], (DEPTH, DIFF_DIM), 0.1)
    lambda_k1 = _normal(ks[7], (DEPTH, DIFF_DIM), 0.1)
    lambda_q2 = _normal(ks[8], (DEPTH, DIFF_DIM), 0.1)
    lambda_k2 = _normal(ks[9], (DEPTH, DIFF_DIM), 0.1)
    diff_norm_w = 1.0 + _normal(ks[10], (DEPTH, HEAD_DIM), 0.01)
    w_branch = _normal(ks[11], (DEPTH, N_BRANCHES, BRANCH_WIDTH, D_MODEL), BRANCH_WIDTH ** -0.5 * BETA)
    w_out = _normal(ks[12], (DEPTH, D_MODEL, D_MODEL), d_sc * BETA)
    ln1_g = 1.0 + _normal(ks[13], (DEPTH, D_MODEL), 0.01)
    ln1_b = _normal(ks[14], (DEPTH, D_MODEL), 0.01)
    w_router = _normal(ks[15], (DEPTH, D_MODEL, N_EXPERTS), d_sc)
    w_gate = _normal(ks[16], (DEPTH, N_EXPERTS, D_MODEL, EXPERT_FF), d_sc * BETA)
    w_up = _normal(ks[17], (DEPTH, N_EXPERTS, D_MODEL, EXPERT_FF), d_sc * BETA)
    w_down = _normal(ks[18], (DEPTH, N_EXPERTS, EXPERT_FF, D_MODEL), EXPERT_FF ** -0.5 * BETA)
    k_ln2g, k_ln2b = jax.random.split(ks[19])
    ln2_g = 1.0 + _normal(k_ln2g, (DEPTH, D_MODEL), 0.01)
    ln2_b = _normal(k_ln2b, (DEPTH, D_MODEL), 0.01)
    return {"x": x, "w_in": w_in, "lambda_q1": lambda_q1, "lambda_k1": lambda_k1,
            "lambda_q2": lambda_q2, "lambda_k2": lambda_k2, "diff_norm_w": diff_norm_w,
            "w_branch": w_branch, "w_out": w_out, "ln1_g": ln1_g, "ln1_b": ln1_b,
            "w_router": w_router, "w_gate": w_gate, "w_up": w_up, "w_down": w_down,
            "ln2_g": ln2_g, "ln2_b": ln2_b}


def reference(x, w_in, lambda_q1, lambda_k1, lambda_q2, lambda_k2, diff_norm_w,
              w_branch, w_out, ln1_g, ln1_b, w_router, w_gate, w_up, w_down, ln2_g, ln2_b):
    b_, S, D = x.shape
    pos = jnp.arange(S)
    o1 = A_QKV
    o2 = 2 * A_QKV
    o3 = 3 * A_QKV
    o4 = o3 + B_QK
    o5 = o4 + B_QK
    o6 = o5 + B_V
    n_groups = len(DIL_GROUPS)
    for l in range(DEPTH):
        lambda_init = 0.8 - 0.6 * math.exp(-0.3 * l)
        z = jnp.einsum('bsd,dc->bsc', x, w_in[l])
        qa = z[..., :o1].reshape(b_, S, n_groups, DIL_HEADS_PER_GROUP, HEAD_DIM)
        ka = z[..., o1:o2].reshape(b_, S, n_groups, DIL_HEADS_PER_GROUP, HEAD_DIM)
        va = z[..., o2:o3].reshape(b_, S, n_groups, DIL_HEADS_PER_GROUP, HEAD_DIM)
        qb = z[..., o3:o4].reshape(b_, S, N_DIFF_HEADS, 2, DIFF_DIM)
        kb = z[..., o4:o5].reshape(b_, S, N_DIFF_HEADS, 2, DIFF_DIM)
        vb = z[..., o5:o6].reshape(b_, S, N_DIFF_HEADS, HEAD_DIM)
        gate = jax.nn.sigmoid(z[..., o6:].reshape(b_, S, N_BRANCHES, D))
        out_a = dilated_attention(rope_partial(qa, pos), rope_partial(ka, pos), va)
        lam = (jnp.exp(jnp.sum(lambda_q1[l].astype(jnp.float32) * lambda_k1[l].astype(jnp.float32)))
               - jnp.exp(jnp.sum(lambda_q2[l].astype(jnp.float32) * lambda_k2[l].astype(jnp.float32)))
               + lambda_init)
        out_b = diff_attention(rope_partial(qb, pos), rope_partial(kb, pos), vb, lam, diff_norm_w[l], lambda_init)
        branches = jnp.stack([out_a.reshape(b_, S, BRANCH_WIDTH), out_b.reshape(b_, S, BRANCH_WIDTH)], axis=2)
        branch_d = jnp.einsum('bsgc,gcd->bsgd', branches, w_branch[l])
        merged = jnp.sum(gate * branch_d, axis=2)
        mix = jnp.einsum('bsd,de->bse', merged, w_out[l])
        x = layer_norm(ALPHA * x + mix, ln1_g[l], ln1_b[l])
        y = expert_choice_ffn(x, w_router[l], w_gate[l], w_up[l], w_down[l])
        x = layer_norm(ALPHA * x + y, ln2_g[l], ln2_b[l])
    return x
```

```python
import functools
import math

import jax
import jax.numpy as jnp
from jax import lax
from jax.experimental import pallas as pl
from jax.experimental.pallas import tpu as pltpu

F32 = jnp.float32
BF16 = jnp.bfloat16

HEAD_DIM = 128
DIL_GROUPS = ((128, 1), (512, 4), (2048, 16))
HEADS_PER_GROUP = 4
N_GROUPS = len(DIL_GROUPS)
N_DIFF_HEADS = 4
BRANCH_WIDTH = HEADS_PER_GROUP * HEAD_DIM
ROPE_THETA = 500000.0
ROPE_FRACTION = 4
EC_CAPACITY = 2
LN_EPS = 1e-5
DIFF_NORM_EPS = 1e-5
NEG_INF = -1e30
LAMBDA_INIT = 0.8 - 0.6 * math.exp(-0.3 * 0)
ALPHA = 2.0 ** 0.25

N_DIL_TILES = 3 * N_GROUPS
QB_TILE, KB_TILE, VB_TILE = N_DIL_TILES, N_DIL_TILES + 1, N_DIL_TILES + 2
GATE_TILE = N_DIL_TILES + 3
LANES = 128
AFF_PAD = LANES
BAND_TQ = 128
VMEM_LIMIT = 56 * 1024 * 1024


def _cparams(sem):
    return pltpu.CompilerParams(dimension_semantics=sem, vmem_limit_bytes=VMEM_LIMIT)


def _rope_tables(seq, dh):
    rot = dh // ROPE_FRACTION
    half = rot // 2
    inv_freq = ROPE_THETA ** (-2.0 * jnp.arange(half, dtype=F32) / rot)
    ang = jnp.arange(seq).astype(F32)[:, None] * inv_freq[None, :]
    cos, sin = jnp.cos(ang), jnp.sin(ang)
    one = jnp.ones((seq, dh - rot), F32)
    zero = lambda n: jnp.zeros((seq, n), F32)
    c = jnp.concatenate([cos, cos, one], axis=-1)
    s_neg = jnp.concatenate([-sin, zero(dh - half)], axis=-1)
    s_pos = jnp.concatenate([zero(half), sin, zero(dh - rot)], axis=-1)
    tab = jnp.stack([c, s_neg, s_pos])
    return jnp.tile(tab, (1, 1, LANES // dh)), half


def _rope_slab(a, tab_ref, half):
    return (a * tab_ref[0] + pltpu.roll(a, LANES - half, 1) * tab_ref[1]
            + pltpu.roll(a, half, 1) * tab_ref[2])


def _slabs(acc):
    return [acc[:, h * LANES:(h + 1) * LANES] for h in range(acc.shape[1] // LANES)]


def _in_proj_kernel(x_ref, w_ref, ra_ref, rb_ref, zn_ref, zd1_ref, zd2_ref, xb_ref, slab_ref, *, half_a, half_b):
    j = pl.program_id(1)

    @pl.when(j == 0)
    def _():
        xb_ref[...] = x_ref[...].astype(BF16)

    acc = jnp.dot(xb_ref[...], w_ref[...].astype(BF16), preferred_element_type=F32)
    is_dil = j < N_DIL_TILES
    grp = j % N_GROUPS
    dil_rope = j < 2 * N_GROUPS
    is_b = jnp.logical_or(j == QB_TILE, j == KB_TILE)

    def rope_a(do_rope):
        return [_rope_slab(a, ra_ref, half_a) if do_rope else a for a in _slabs(acc)]

    def to_natural(slabs):
        zn_ref[...] = jnp.concatenate(slabs, axis=1).astype(BF16)

    def to_phase_major(slabs, out_ref, r):
        for h, s in enumerate(slabs):
            slab_ref[h] = s
        rows = slab_ref.shape[1] // r
        for p in range(r):
            out_ref[0, p] = jnp.concatenate(
                [slab_ref[h, pl.ds(p, rows, stride=r), :] for h in range(len(slabs))], axis=1).astype(BF16)

    for do_rope, cond in ((True, dil_rope), (False, jnp.logical_not(dil_rope))):
        @pl.when(is_dil & cond & (grp == 0))
        def _(do_rope=do_rope):
            to_natural(rope_a(do_rope))

        @pl.when(is_dil & cond & (grp == 1))
        def _(do_rope=do_rope):
            to_phase_major(rope_a(do_rope), zd1_ref, DIL_GROUPS[1][1])

        @pl.when(is_dil & cond & (grp == 2))
        def _(do_rope=do_rope):
            to_phase_major(rope_a(do_rope), zd2_ref, DIL_GROUPS[2][1])

    @pl.when(is_b)
    def _():
        to_natural([_rope_slab(a, rb_ref, half_b) for a in _slabs(acc)])

    @pl.when(jnp.logical_not(is_dil | is_b))
    def _():
        zn_ref[...] = acc.astype(BF16)


def _zn_tile(n_gate_tiles):
    return dict(qa=n_gate_tiles, ka=n_gate_tiles + 1, va=n_gate_tiles + 2,
                qb=n_gate_tiles + 3, kb=n_gate_tiles + 4, vb=n_gate_tiles + 5)


def _in_proj(x2, w, batch, seq, tm=1024):
    t, d = x2.shape
    n_tiles = w.shape[1] // BRANCH_WIDTH
    n_gate_tiles = n_tiles - GATE_TILE
    tn = BRANCH_WIDTH
    tm = min(tm, seq)
    spt = seq // tm
    ra, half_a = _rope_tables(seq, HEAD_DIM)
    rb, half_b = _rope_tables(seq, HEAD_DIM // 2)
    r1, r2 = DIL_GROUPS[1][1], DIL_GROUPS[2][1]
    zt = _zn_tile(n_gate_tiles)

    def zn_col(j):
        return jnp.where(j >= GATE_TILE, j - GATE_TILE,
                         jnp.where(j >= N_DIL_TILES, j - N_DIL_TILES + zt["qb"], zt["qa"] + j // N_GROUPS))

    def zd_col(g):
        return lambda j: jnp.clip((j - g) // N_GROUPS, 0, 2)

    return pl.pallas_call(
        functools.partial(_in_proj_kernel, half_a=half_a, half_b=half_b),
        out_shape=(
            jax.ShapeDtypeStruct((t, (n_gate_tiles + 6) * tn), BF16),
            jax.ShapeDtypeStruct((batch, r1, seq // r1, 3 * tn), BF16),
            jax.ShapeDtypeStruct((batch, r2, seq // r2, 3 * tn), BF16),
        ),
        grid=(t // tm, n_tiles),
        in_specs=[
            pl.BlockSpec((tm, d), lambda i, j: (i, 0)),
            pl.BlockSpec((d, tn), lambda i, j: (0, j)),
            pl.BlockSpec((3, tm, LANES), lambda i, j: (0, i % spt, 0)),
            pl.BlockSpec((3, tm, LANES), lambda i, j: (0, i % spt, 0)),
        ],
        out_specs=(
            pl.BlockSpec((tm, tn), lambda i, j: (i, zn_col(j))),
            pl.BlockSpec((1, r1, tm // r1, tn), lambda i, j: (i // spt, 0, i % spt, zd_col(1)(j))),
            pl.BlockSpec((1, r2, tm // r2, tn), lambda i, j: (i // spt, 0, i % spt, zd_col(2)(j))),
        ),
        scratch_shapes=[pltpu.VMEM((tm, d), BF16), pltpu.VMEM((tn // LANES, tm, LANES), F32)],
        compiler_params=_cparams(("arbitrary", "arbitrary")),
        name="in_proj",
    )(x2, w, ra, rb)


def _band_kernel(q_ref, k_ref, v_ref, o_ref, lse_ref, *, seq_l, tl, r, half, scale):
    li = pl.program_id(1)
    win = BAND_TQ + 2 * half
    row = lax.broadcasted_iota(jnp.int32, (BAND_TQ, win), 0)
    col = lax.broadcasted_iota(jnp.int32, (BAND_TQ, win), 1)
    tiles = tl // BAND_TQ

    def body(it, carry):
        p = it // tiles
        t = it % tiles
        r0 = pl.multiple_of(t * BAND_TQ, BAND_TQ)
        q0 = li * tl + r0
        ws = pl.multiple_of(jnp.clip(q0 - half, 0, seq_l - win), half)
        valid = jnp.abs((q0 + row) - (ws + col)) <= half
        for h in range(HEADS_PER_GROUP):
            hs = slice(h * HEAD_DIM, (h + 1) * HEAD_DIM)
            q = q_ref[0, p, pl.ds(r0, BAND_TQ), hs]
            k = k_ref[0, p, pl.ds(ws, win), hs]
            v = v_ref[0, p, pl.ds(ws, win), hs]
            s = lax.dot_general(q, k, (((1,), (1,)), ((), ())), preferred_element_type=F32) * scale
            s = jnp.where(valid, s, NEG_INF)
            m = jnp.max(s, axis=-1, keepdims=True)
            pr = jnp.exp(s - m)
            den = jnp.sum(pr, axis=-1, keepdims=True)
            acc = jnp.dot(pr.astype(BF16), v, preferred_element_type=F32)
            dst = pl.ds(r0 * r + p, BAND_TQ, stride=r) if r > 1 else pl.ds(r0, BAND_TQ)
            o_ref[h, dst, :] = acc / den
            lse_ref[h, dst, :] = jnp.broadcast_to(m + jnp.log(den), (BAND_TQ, HEAD_DIM))
        return carry

    lax.fori_loop(0, r * tiles, body, 0)


def _band_attention(zsrc, tiles_qkv, batch, seq, g):
    window, r = DIL_GROUPS[g]
    half = window // (2 * r)
    seq_l = seq // r
    tl = max(BAND_TQ, 512 // r)
    tq_, tk_, tv_ = tiles_qkv
    t = batch * seq
    nl = seq_l // tl
    out_sd = jax.ShapeDtypeStruct((HEADS_PER_GROUP, t, HEAD_DIM), F32)
    o_spec = pl.BlockSpec((HEADS_PER_GROUP, tl * r, HEAD_DIM), lambda b, li: (0, b * nl + li, 0))
    return pl.pallas_call(
        functools.partial(_band_kernel, seq_l=seq_l, tl=tl, r=r, half=half, scale=HEAD_DIM ** -0.5),
        out_shape=(out_sd, out_sd),
        grid=(batch, nl),
        in_specs=[
            pl.BlockSpec((1, r, tl, BRANCH_WIDTH), lambda b, li: (b, 0, li, tq_)),
            pl.BlockSpec((1, r, seq_l, BRANCH_WIDTH), lambda b, li: (b, 0, 0, tk_)),
            pl.BlockSpec((1, r, seq_l, BRANCH_WIDTH), lambda b, li: (b, 0, 0, tv_)),
        ],
        out_specs=(o_spec, o_spec),
        compiler_params=_cparams(("arbitrary", "arbitrary")),
        name=f"band_attn_g{g}",
    )(zsrc, zsrc, zsrc)


def _diff_kernel(lq1_ref, lk1_ref, lq2_ref, lk2_ref, nw_ref, q_ref, k_ref, v_ref, o_ref, *, seq, tk):
    lam = (jnp.exp(jnp.sum(lq1_ref[...] * lk1_ref[...])) - jnp.exp(jnp.sum(lq2_ref[...] * lk2_ref[...]))
           + LAMBDA_INIT)
    dc = HEAD_DIM // 2
    q = q_ref[0] * jnp.asarray(dc ** -0.5, BF16)
    tq = q.shape[0]
    lane = lax.broadcasted_iota(jnp.int32, (tq, HEAD_DIM), 1)
    qc = [jnp.where(lane < dc, q, jnp.zeros_like(q)), jnp.where(lane >= dc, q, jnp.zeros_like(q))]

    def body(j, carry):
        k0 = pl.multiple_of(j * tk, tk)
        kj = k_ref[0, pl.ds(k0, tk), :]
        vj = v_ref[0, pl.ds(k0, tk), :]
        out = []
        for c in range(2):
            m, l, acc = carry[c]
            s = lax.dot_general(qc[c], kj, (((1,), (1,)), ((), ())), preferred_element_type=F32)
            m_new = jnp.maximum(m, jnp.max(s, axis=-1, keepdims=True))
            a = jnp.exp(m - m_new)
            p = jnp.exp(s - m_new)
            l = a * l + jnp.sum(p, axis=-1, keepdims=True)
            acc = a * acc + jnp.dot(p.astype(BF16), vj, preferred_element_type=F32)
            out.append((m_new, l, acc))
        return tuple(out)

    one = (jnp.full((tq, 1), -jnp.inf, F32), jnp.zeros((tq, 1), F32), jnp.zeros((tq, HEAD_DIM), F32))
    (_, l0, acc0), (_, l1, acc1) = lax.fori_loop(0, seq // tk, body, (one, one), unroll=2)
    o = acc0 / l0 - lam * (acc1 / l1)
    o = o * lax.rsqrt(jnp.mean(o * o, axis=-1, keepdims=True) + DIFF_NORM_EPS)
    o_ref[0] = o * nw_ref[...] * (1.0 - LAMBDA_INIT)


def _diff_attention(zn, zt, batch, seq, lq1, lk1, lq2, lk2, norm_w, tq=512, tk=512):
    zv = zn.reshape(batch, seq, zn.shape[1])
    hpt = BRANCH_WIDTH // HEAD_DIM
    vec = lambda n: pl.BlockSpec((1, n), lambda b, h, qi: (0, 0))
    return pl.pallas_call(
        functools.partial(_diff_kernel, seq=seq, tk=tk),
        out_shape=jax.ShapeDtypeStruct((batch, seq, BRANCH_WIDTH), F32),
        grid=(batch, N_DIFF_HEADS, seq // tq),
        in_specs=[
            vec(HEAD_DIM // 2), vec(HEAD_DIM // 2), vec(HEAD_DIM // 2), vec(HEAD_DIM // 2), vec(HEAD_DIM),
            pl.BlockSpec((1, tq, HEAD_DIM), lambda b, h, qi: (b, qi, zt["qb"] * hpt + h)),
            pl.BlockSpec((1, seq, HEAD_DIM), lambda b, h, qi: (b, 0, zt["kb"] * hpt + h)),
            pl.BlockSpec((1, seq, HEAD_DIM), lambda b, h, qi: (b, 0, zt["vb"] * hpt + h)),
        ],
        out_specs=pl.BlockSpec((1, tq, HEAD_DIM), lambda b, h, qi: (b, qi, h)),
        compiler_params=_cparams(("arbitrary", "arbitrary", "arbitrary")),
        name="diff_attn",
    )(lq1, lk1, lq2, lk2, norm_w, zv, zv, zv).reshape(batch * seq, BRANCH_WIDTH)


def _layer_norm(h, g, b):
    mu = jnp.mean(h, axis=-1, keepdims=True)
    hc = h - mu
    var = jnp.mean(hc * hc, axis=-1, keepdims=True)
    return hc * lax.rsqrt(var + LN_EPS) * g + b


def _post_kernel(o0_ref, o1_ref, o2_ref, l0_ref, l1_ref, l2_ref, ob_ref, g0_ref, g1_ref, x_ref,
                 wb_ref, wo_ref, lng_ref, lnb_ref, wr_ref, x1_ref, yext_ref, *, n_exp):
    d = x_ref.shape[1]
    heads = lambda ref: jnp.concatenate([ref[h] for h in range(HEADS_PER_GROUP)], axis=1)
    l0, l1, l2 = heads(l0_ref), heads(l1_ref), heads(l2_ref)
    lm = jnp.maximum(jnp.maximum(l0, l1), l2)
    e0, e1, e2 = jnp.exp(l0 - lm), jnp.exp(l1 - lm), jnp.exp(l2 - lm)
    den = e0 + e1 + e2
    oa = heads(o0_ref) * (e0 / den) + heads(o1_ref) * (e1 / den) + heads(o2_ref) * (e2 / den)
    bd0 = jnp.dot(oa.astype(BF16), wb_ref[0], preferred_element_type=F32)
    bd1 = jnp.dot(ob_ref[...].astype(BF16), wb_ref[1], preferred_element_type=F32)
    merged = jax.nn.sigmoid(g0_ref[...].astype(F32)) * bd0 + jax.nn.sigmoid(g1_ref[...].astype(F32)) * bd1
    mix = jnp.dot(merged.astype(BF16), wo_ref[...], preferred_element_type=F32)
    x1 = _layer_norm(ALPHA * x_ref[...] + mix, lng_ref[...], lnb_ref[...])
    logits = jnp.dot(x1.astype(BF16), wr_ref[...], preferred_element_type=F32)
    lane = lax.broadcasted_iota(jnp.int32, logits.shape, 1)
    logits = jnp.where(lane < n_exp, logits, -jnp.inf)
    ex = jnp.exp(logits - jnp.max(logits, axis=-1, keepdims=True))
    aff = ex / jnp.sum(ex, axis=-1, keepdims=True)
    x1_ref[...] = x1
    yext_ref[:, :d] = ALPHA * x1
    yext_ref[:, d:] = aff


def _post(o, lse, ob, zn, x2, wb, wo, lng, lnb, wr, n_exp, tm=256):
    t, d = x2.shape
    row = lambda w: pl.BlockSpec((tm, w), lambda i: (i, 0))
    hm = pl.BlockSpec((HEADS_PER_GROUP, tm, HEAD_DIM), lambda i: (0, i, 0))
    full = lambda a: pl.BlockSpec(a.shape, lambda i: (0,) * a.ndim)
    return pl.pallas_call(
        functools.partial(_post_kernel, n_exp=n_exp),
        out_shape=(jax.ShapeDtypeStruct((t, d), F32), jax.ShapeDtypeStruct((t, d + AFF_PAD), F32)),
        grid=(t // tm,),
        in_specs=[hm] * 6 + [
            row(BRANCH_WIDTH),
            pl.BlockSpec((tm, d), lambda i: (i, 0)),
            pl.BlockSpec((tm, d), lambda i: (i, 1)),
            row(d), full(wb), full(wo), full(lng), full(lnb), full(wr),
        ],
        out_specs=(row(d), row(d + AFF_PAD)),
        compiler_params=_cparams(("arbitrary",)),
        name="post_attn",
    )(o[0], o[1], o[2], lse[0], lse[1], lse[2], ob, zn, zn, x2, wb, wo, lng, lnb, wr)


def _topk_kernel(aff_ref, idx_ref, eqcs_ref, cs_ref, *, seq, cap, n_exp):
    b = pl.program_id(0)
    aff = aff_ref[...]

    def search(i, t):
        cand = t | jnp.left_shift(jnp.int32(1), 30 - i)
        cnt = jnp.sum((aff >= pltpu.bitcast(cand, F32)).astype(F32), axis=0, keepdims=True)
        return jnp.where(cnt >= cap, cand, t)

    thr = pltpu.bitcast(lax.fori_loop(0, 31, search, jnp.zeros((1, LANES), jnp.int32)), F32)
    need = cap - jnp.sum((aff > thr).astype(F32), axis=0, keepdims=True)

    tri = (lax.broadcasted_iota(jnp.int32, (LANES, LANES), 1)
           <= lax.broadcasted_iota(jnp.int32, (LANES, LANES), 0)).astype(BF16)

    def cumsum_rows(mask_fn, out_ref):
        def blk(i, off):
            r0 = pl.multiple_of(i * LANES, LANES)
            part = jnp.dot(tri, mask_fn(r0).astype(BF16), preferred_element_type=F32) + off
            out_ref[pl.ds(r0, LANES), :] = part
            return part[LANES - 1:LANES, :]
        lax.fori_loop(0, seq // LANES, blk, jnp.zeros((1, LANES), F32))

    cumsum_rows(lambda r0: aff_ref[pl.ds(r0, LANES), :] == thr, eqcs_ref)

    def selected(r0):
        a = aff_ref[pl.ds(r0, LANES), :]
        return (a > thr) | ((a == thr) & (eqcs_ref[pl.ds(r0, LANES), :] <= need))

    cumsum_rows(selected, cs_ref)

    chunk = 512
    slot = lax.broadcasted_iota(jnp.int32, (1, cap), 1).astype(F32)
    lane = lax.broadcasted_iota(jnp.int32, (chunk, LANES), 1)
    rows = []
    for e in range(n_exp):
        def count(i, tot, e=e):
            r0 = pl.multiple_of(i * chunk, chunk)
            col = jnp.sum(jnp.where(lane == e, cs_ref[pl.ds(r0, chunk), :], 0.0), axis=1, keepdims=True)
            return tot + jnp.sum((col <= slot).astype(F32), axis=0, keepdims=True)
        rows.append(lax.fori_loop(0, seq // chunk, count, jnp.zeros((1, cap), F32)))
    idx_ref[0] = jnp.concatenate(rows, axis=0).astype(jnp.int32) + b * seq


def _topk(yext, batch, seq, d, n_exp, cap):
    return pl.pallas_call(
        functools.partial(_topk_kernel, seq=seq, cap=cap, n_exp=n_exp),
        out_shape=jax.ShapeDtypeStruct((batch, n_exp, cap), jnp.int32),
        grid=(batch,),
        in_specs=[pl.BlockSpec((seq, AFF_PAD), lambda b: (b, d // AFF_PAD))],
        out_specs=pl.BlockSpec((1, n_exp, cap), lambda b: (b, 0, 0)),
        scratch_shapes=[pltpu.VMEM((seq, LANES), F32), pltpu.VMEM((seq, LANES), F32)],
        compiler_params=_cparams(("arbitrary",)),
        name="expert_topk",
    )(yext)


def _ffn_kernel(idx_ref, x1_hbm, yin_hbm, wg_ref, wu_ref, wd_ref, yext_hbm, xg_ref, xb_ref, yg_ref, acc_ref, sems,
                *, rows, d, n_split, nf):
    del yin_hbm
    e, hf, f = pl.program_id(0), pl.program_id(1), pl.program_id(2)
    n_groups = pl.num_programs(0) * n_split
    grp = e * n_split + hf
    base = grp * rows
    nxt = jnp.minimum(grp + 1, n_groups - 1) * rows
    per = rows // nf

    def x_copy(r, tok):
        return pltpu.make_async_copy(x1_hbm.at[pl.ds(tok, 1)], xg_ref.at[pl.ds(r, 1)], sems.at[0])

    def y_in_copy(r, tok):
        return pltpu.make_async_copy(yext_hbm.at[pl.ds(tok, 1)], yg_ref.at[pl.ds(r, 1)], sems.at[1])

    def y_out_copy(r, tok):
        return pltpu.make_async_copy(yg_ref.at[pl.ds(r, 1)], yext_hbm.at[pl.ds(tok, 1)], sems.at[2])

    def for_rows(fn, start):
        def body(r, c):
            fn(r, idx_ref[start + r])
            return c
        lax.fori_loop(0, rows, body, 0, unroll=8)

    @pl.when((f == 0) & (grp == 0))
    def _():
        for_rows(lambda r, tok: x_copy(r, tok).start(), base)

    @pl.when(f == 0)
    def _():
        for_rows(lambda r, tok: x_copy(r, tok).wait(), base)
        xb_ref[...] = xg_ref[...].astype(BF16)
        acc_ref[...] = jnp.zeros_like(acc_ref)

    for r in range(per):
        row = f * per + r
        x_copy(row, idx_ref[nxt + row]).start()
        y_in_copy(row, idx_ref[base + row]).start()

    x = xb_ref[...]
    g = jnp.dot(x, wg_ref[0].astype(BF16), preferred_element_type=F32)
    u = jnp.dot(x, wu_ref[0].astype(BF16), preferred_element_type=F32)
    hid = (g * jax.nn.sigmoid(g)) * u
    acc_ref[...] += jnp.dot(hid.astype(BF16), wd_ref[0].astype(BF16), preferred_element_type=F32)

    @pl.when(f == nf - 1)
    def _():
        for_rows(lambda r, tok: y_in_copy(r, tok).wait(), base)
        aff = yg_ref[:, d:]
        lane = lax.broadcasted_iota(jnp.int32, aff.shape, 1)
        gate = jnp.sum(jnp.where(lane == e, aff, 0.0), axis=-1, keepdims=True)
        yg_ref[:, :d] = yg_ref[:, :d] + acc_ref[...] * gate
        for_rows(lambda r, tok: y_out_copy(r, tok).start(), base)
        for_rows(lambda r, tok: y_out_copy(r, tok).wait(), base)

    @pl.when((f == nf - 1) & (grp == n_groups - 1))
    def _():
        for_rows(lambda r, tok: x_copy(r, tok).wait(), base)


def _expert_ffn(idx_flat, x1, yext, w_gate, w_up, w_down, tf=256, n_split=2):
    t, d = x1.shape
    n_exp, _, ff = w_gate.shape
    tf = min(tf, ff)
    nf = ff // tf
    rows = idx_flat.shape[0] // (n_exp * n_split)
    assert rows % nf == 0
    grid_spec = pltpu.PrefetchScalarGridSpec(
        num_scalar_prefetch=1,
        grid=(n_exp, n_split, nf),
        in_specs=[
            pl.BlockSpec(memory_space=pl.ANY),
            pl.BlockSpec(memory_space=pl.ANY),
            pl.BlockSpec((1, d, tf), lambda e, h, f, idx: (e, 0, f)),
            pl.BlockSpec((1, d, tf), lambda e, h, f, idx: (e, 0, f)),
            pl.BlockSpec((1, tf, d), lambda e, h, f, idx: (e, f, 0)),
        ],
        out_specs=pl.BlockSpec(memory_space=pl.ANY),
        scratch_shapes=[
            pltpu.VMEM((rows, d), F32),
            pltpu.VMEM((rows, d), BF16),
            pltpu.VMEM((rows, d + AFF_PAD), F32),
            pltpu.VMEM((rows, d), F32),
            pltpu.SemaphoreType.DMA((3,)),
        ],
    )
    return pl.pallas_call(
        functools.partial(_ffn_kernel, rows=rows, d=d, n_split=n_split, nf=nf),
        out_shape=jax.ShapeDtypeStruct(yext.shape, yext.dtype),
        grid_spec=grid_spec,
        input_output_aliases={2: 0},
        compiler_params=_cparams(("arbitrary", "arbitrary", "arbitrary")),
        name="expert_ffn",
    )(idx_flat, x1, yext, w_gate, w_up, w_down)


def _ln2_kernel(y_ref, g_ref, b_ref, o_ref):
    o_ref[...] = _layer_norm(y_ref[...], g_ref[...], b_ref[...])


def _ln2(yext, d, g, b, tm=512):
    t = yext.shape[0]
    return pl.pallas_call(
        _ln2_kernel,
        out_shape=jax.ShapeDtypeStruct((t, d), F32),
        grid=(t // tm,),
        in_specs=[pl.BlockSpec((tm, d), lambda i: (i, 0)),
                  pl.BlockSpec((1, d), lambda i: (0, 0)), pl.BlockSpec((1, d), lambda i: (0, 0))],
        out_specs=pl.BlockSpec((tm, d), lambda i: (i, 0)),
        compiler_params=_cparams(("arbitrary",)),
        name="ln2",
    )(yext, g, b)


def kernel(x, w_in, lambda_q1, lambda_k1, lambda_q2, lambda_k2, diff_norm_w, w_branch, w_out, ln1_g, ln1_b,
           w_router, w_gate, w_up, w_down, ln2_g, ln2_b):
    batch, seq, d = x.shape
    assert w_in.shape[0] == 1, "one layer"
    n_exp = w_router.shape[-1]
    cap = EC_CAPACITY * seq // n_exp
    x2 = x.reshape(batch * seq, d)

    zn, zd1, zd2 = _in_proj(x2, w_in[0], batch, seq)
    zt = _zn_tile(zn.shape[1] // BRANCH_WIDTH - 6)
    zn4 = zn.reshape(batch, 1, seq, zn.shape[1])
    o, lse = zip(_band_attention(zn4, (zt["qa"], zt["ka"], zt["va"]), batch, seq, 0),
                 _band_attention(zd1, (0, 1, 2), batch, seq, 1),
                 _band_attention(zd2, (0, 1, 2), batch, seq, 2))
    ob = _diff_attention(zn, zt, batch, seq, lambda_q1, lambda_k1, lambda_q2, lambda_k2, diff_norm_w)

    wr = jnp.pad(w_router[0], ((0, 0), (0, AFF_PAD - n_exp))).astype(BF16)
    x1, yext = _post(o, lse, ob, zn, x2, w_branch[0].astype(BF16), w_out[0].astype(BF16),
                     ln1_g, ln1_b, wr, n_exp)
    idx = _topk(yext, batch, seq, d, n_exp, cap)
    idx_flat = idx.transpose(1, 0, 2).reshape(-1)
    yext = _expert_ffn(idx_flat, x1, yext, w_gate[0], w_up[0], w_down[0])
    return _ln2(yext, d, ln2_g, ln2_b).reshape(batch, seq, d)
```

```python
import functools
import math

import jax
import jax.numpy as jnp
from jax import lax
from jax.experimental import pallas as pl
from jax.experimental.pallas import tpu as pltpu

F32 = jnp.float32
BF16 = jnp.bfloat16

HEAD_DIM = 128
DIL_GROUPS = ((128, 1), (512, 4), (2048, 16))
HEADS_PER_GROUP = 4
N_GROUPS = len(DIL_GROUPS)
N_DIFF_HEADS = 4
BRANCH_WIDTH = HEADS_PER_GROUP * HEAD_DIM
ROPE_THETA = 500000.0
ROPE_FRACTION = 4
EC_CAPACITY = 2
LN_EPS = 1e-5
DIFF_NORM_EPS = 1e-5
NEG_INF = -1e30
LAMBDA_INIT = 0.8 - 0.6 * math.exp(-0.3 * 0)
ALPHA = 2.0 ** 0.25
DIFF_Q_SCALE = (HEAD_DIM // 2) ** -0.5 * math.log2(math.e)

N_DIL_TILES = 3 * N_GROUPS
QB_TILE, KB_TILE, VB_TILE = N_DIL_TILES, N_DIL_TILES + 1, N_DIL_TILES + 2
GATE_TILE = N_DIL_TILES + 3
LANES = 128
AFF_PAD = LANES
BAND_TQ = 128
VMEM_LIMIT = 56 * 1024 * 1024


def _cparams(sem):
    return pltpu.CompilerParams(dimension_semantics=sem, vmem_limit_bytes=VMEM_LIMIT)


def _rope_tables(seq, dh):
    rot = dh // ROPE_FRACTION
    half = rot // 2
    inv_freq = ROPE_THETA ** (-2.0 * jnp.arange(half, dtype=F32) / rot)
    ang = jnp.arange(seq).astype(F32)[:, None] * inv_freq[None, :]
    cos, sin = jnp.cos(ang), jnp.sin(ang)
    one = jnp.ones((seq, dh - rot), F32)
    zero = lambda n: jnp.zeros((seq, n), F32)
    c = jnp.concatenate([cos, cos, one], axis=-1)
    s_neg = jnp.concatenate([-sin, zero(dh - half)], axis=-1)
    s_pos = jnp.concatenate([zero(half), sin, zero(dh - rot)], axis=-1)
    tab = jnp.stack([c, s_neg, s_pos])
    return jnp.tile(tab, (1, 1, LANES // dh)), half


def _rope_slab(a, tab_ref, half):
    return (a * tab_ref[0] + pltpu.roll(a, LANES - half, 1) * tab_ref[1]
            + pltpu.roll(a, half, 1) * tab_ref[2])


def _slabs(acc):
    return [acc[:, h * LANES:(h + 1) * LANES] for h in range(acc.shape[1] // LANES)]


def _in_proj_kernel(x_ref, w_ref, ra_ref, rb_ref, zn_ref, zd1_ref, zd2_ref, xb_ref, slab_ref, *, half_a, half_b):
    j = pl.program_id(1)

    @pl.when(j == 0)
    def _():
        xb_ref[...] = x_ref[...].astype(BF16)

    is_dil = j < N_DIL_TILES
    grp = j % N_GROUPS
    dil_rope = j < 2 * N_GROUPS
    is_b = jnp.logical_or(j == QB_TILE, j == KB_TILE)

    def project():
        return jnp.dot(xb_ref[...], w_ref[...].astype(BF16), preferred_element_type=F32)

    def rope_a(do_rope):
        return [_rope_slab(a, ra_ref, half_a) if do_rope else a for a in _slabs(project())]

    def to_natural(slabs):
        zn_ref[...] = jnp.concatenate(slabs, axis=1).astype(BF16)

    def to_phase_major(slabs, out_ref, r):
        for h, s in enumerate(slabs):
            slab_ref[h] = s
        rows = slab_ref.shape[1] // r
        for p in range(r):
            out_ref[0, p] = jnp.concatenate(
                [slab_ref[h, pl.ds(p, rows, stride=r), :] for h in range(len(slabs))], axis=1).astype(BF16)

    @pl.when(is_dil & dil_rope & (grp == 0))
    def _():
        to_natural(rope_a(True))

    for do_rope, cond in ((True, dil_rope), (False, jnp.logical_not(dil_rope))):
        @pl.when(is_dil & cond & (grp == 1))
        def _(do_rope=do_rope):
            to_phase_major(rope_a(do_rope), zd1_ref, DIL_GROUPS[1][1])

        @pl.when(is_dil & cond & (grp == 2))
        def _(do_rope=do_rope):
            to_phase_major(rope_a(do_rope), zd2_ref, DIL_GROUPS[2][1])

    @pl.when(is_b)
    def _():
        scale = jnp.where(j == QB_TILE, DIFF_Q_SCALE, 1.0).astype(F32)
        to_natural([_rope_slab(a, rb_ref, half_b) * scale for a in _slabs(project())])

    @pl.when(jnp.logical_not(is_b) & jnp.logical_not(is_dil & (dil_rope | (grp != 0))))
    def _():
        zn_ref[...] = project().astype(BF16)


def _zn_tile(n_gate_tiles):
    return dict(qa=n_gate_tiles, ka=n_gate_tiles + 1, va=n_gate_tiles + 2,
                qb=n_gate_tiles + 3, kb=n_gate_tiles + 4, vb=n_gate_tiles + 5)


def _in_proj(x2, w, batch, seq, tm=1024):
    t, d = x2.shape
    n_tiles = w.shape[1] // BRANCH_WIDTH
    n_gate_tiles = n_tiles - GATE_TILE
    tn = BRANCH_WIDTH
    tm = min(tm, seq)
    spt = seq // tm
    ra, half_a = _rope_tables(seq, HEAD_DIM)
    rb, half_b = _rope_tables(seq, HEAD_DIM // 2)
    r1, r2 = DIL_GROUPS[1][1], DIL_GROUPS[2][1]
    zt = _zn_tile(n_gate_tiles)

    def zn_col(j):
        return jnp.where(j >= GATE_TILE, j - GATE_TILE,
                         jnp.where(j >= N_DIL_TILES, j - N_DIL_TILES + zt["qb"], zt["qa"] + j // N_GROUPS))

    def zd_col(g):
        return lambda j: jnp.clip((j - g) // N_GROUPS, 0, 2)

    return pl.pallas_call(
        functools.partial(_in_proj_kernel, half_a=half_a, half_b=half_b),
        out_shape=(
            jax.ShapeDtypeStruct((t, (n_gate_tiles + 6) * tn), BF16),
            jax.ShapeDtypeStruct((batch, r1, seq // r1, 3 * tn), BF16),
            jax.ShapeDtypeStruct((batch, r2, seq // r2, 3 * tn), BF16),
        ),
        grid=(t // tm, n_tiles),
        in_specs=[
            pl.BlockSpec((tm, d), lambda i, j: (i, 0)),
            pl.BlockSpec((d, tn), lambda i, j: (0, j)),
            pl.BlockSpec((3, tm, LANES), lambda i, j: (0, i % spt, 0)),
            pl.BlockSpec((3, tm, LANES), lambda i, j: (0, i % spt, 0)),
        ],
        out_specs=(
            pl.BlockSpec((tm, tn), lambda i, j: (i, zn_col(j))),
            pl.BlockSpec((1, r1, tm // r1, tn), lambda i, j: (i // spt, 0, i % spt, zd_col(1)(j))),
            pl.BlockSpec((1, r2, tm // r2, tn), lambda i, j: (i // spt, 0, i % spt, zd_col(2)(j))),
        ),
        scratch_shapes=[pltpu.VMEM((tm, d), BF16), pltpu.VMEM((tn // LANES, tm, LANES), F32)],
        compiler_params=_cparams(("arbitrary", "arbitrary")),
        name="in_proj",
    )(x2, w, ra, rb)


def _band_kernel(q_ref, k_ref, v_ref, o_ref, lse_ref, *, seq_l, tl, r, half, scale):
    li = pl.program_id(1)
    win = BAND_TQ + 2 * half
    row = lax.broadcasted_iota(jnp.int32, (BAND_TQ, win), 0)
    col = lax.broadcasted_iota(jnp.int32, (BAND_TQ, win), 1)
    tiles = tl // BAND_TQ

    def body(it, carry):
        p = it // tiles
        t = it % tiles
        r0 = pl.multiple_of(t * BAND_TQ, BAND_TQ)
        q0 = li * tl + r0
        ws = pl.multiple_of(jnp.clip(q0 - half, 0, seq_l - win), half)
        valid = jnp.abs((q0 + row) - (ws + col)) <= half
        for h in range(HEADS_PER_GROUP):
            hs = slice(h * HEAD_DIM, (h + 1) * HEAD_DIM)
            q = q_ref[0, p, pl.ds(r0, BAND_TQ), hs]
            k = k_ref[0, p, pl.ds(ws, win), hs]
            v = v_ref[0, p, pl.ds(ws, win), hs]
            s = lax.dot_general(q, k, (((1,), (1,)), ((), ())), preferred_element_type=F32) * scale
            s = jnp.where(valid, s, NEG_INF)
            m = jnp.max(s, axis=-1, keepdims=True)
            pr = jnp.exp(s - m)
            den = jnp.sum(pr, axis=-1, keepdims=True)
            acc = jnp.dot(pr.astype(BF16), v, preferred_element_type=F32)
            dst = pl.ds(r0 * r + p, BAND_TQ, stride=r) if r > 1 else pl.ds(r0, BAND_TQ)
            o_ref[h, dst, :] = acc / den
            lse_ref[h, dst, :] = jnp.broadcast_to(m + jnp.log(den), (BAND_TQ, HEAD_DIM))
        return carry

    lax.fori_loop(0, r * tiles, body, 0, unroll=4)


def _band_attention(zsrc, tiles_qkv, batch, seq, g):
    window, r = DIL_GROUPS[g]
    half = window // (2 * r)
    seq_l = seq // r
    tl = max(BAND_TQ, 512 // r)
    tq_, tk_, tv_ = tiles_qkv
    t = batch * seq
    nl = seq_l // tl
    out_sd = jax.ShapeDtypeStruct((HEADS_PER_GROUP, t, HEAD_DIM), F32)
    o_spec = pl.BlockSpec((HEADS_PER_GROUP, tl * r, HEAD_DIM), lambda b, li: (0, b * nl + li, 0))
    return pl.pallas_call(
        functools.partial(_band_kernel, seq_l=seq_l, tl=tl, r=r, half=half, scale=HEAD_DIM ** -0.5),
        out_shape=(out_sd, out_sd),
        grid=(batch, nl),
        in_specs=[
            pl.BlockSpec((1, r, tl, BRANCH_WIDTH), lambda b, li: (b, 0, li, tq_)),
            pl.BlockSpec((1, r, seq_l, BRANCH_WIDTH), lambda b, li: (b, 0, 0, tk_)),
            pl.BlockSpec((1, r, seq_l, BRANCH_WIDTH), lambda b, li: (b, 0, 0, tv_)),
        ],
        out_specs=(o_spec, o_spec),
        compiler_params=_cparams(("arbitrary", "arbitrary")),
        name=f"band_attn_g{g}",
    )(zsrc, zsrc, zsrc)


def _diff_kernel(lq1_ref, lk1_ref, lq2_ref, lk2_ref, nw_ref, q_ref, k_ref, v_ref, o_ref, s_ref, *, seq, tk):
    lam = (jnp.exp(jnp.sum(lq1_ref[...] * lk1_ref[...])) - jnp.exp(jnp.sum(lq2_ref[...] * lk2_ref[...]))
           + LAMBDA_INIT)
    dc = HEAD_DIM // 2
    q = q_ref[0]
    tq = q.shape[0]
    lane = lax.broadcasted_iota(jnp.int32, (tq, HEAD_DIM), 1)
    qc = [jnp.where(lane < dc, q, jnp.zeros_like(q)), jnp.where(lane >= dc, q, jnp.zeros_like(q))]
    slabs = tk // LANES

    def scores(j, mx):
        kj = k_ref[0, pl.ds(pl.multiple_of(j * tk, tk), tk), :]
        out = []
        for c in range(2):
            s = lax.dot_general(qc[c], kj, (((1,), (1,)), ((), ())), preferred_element_type=F32)
            s_ref[c, j] = s
            m = mx[c]
            for t in range(slabs):
                m = jnp.maximum(m, s[:, t * LANES:(t + 1) * LANES])
            out.append(m)
        return tuple(out)

    neg = jnp.full((tq, LANES), -jnp.inf, F32)
    mx = lax.fori_loop(0, seq // tk, scores, (neg, neg), unroll=2)
    row_max = [jnp.broadcast_to(jnp.max(m, axis=-1, keepdims=True), (tq, LANES)) for m in mx]

    ones = jnp.ones((tk, LANES), BF16)

    def weigh(j, accs):
        vj = jnp.concatenate([v_ref[0, pl.ds(pl.multiple_of(j * tk, tk), tk), :], ones], axis=1)
        out = []
        for c in range(2):
            p = [jnp.exp2(s_ref[c, j, :, t * LANES:(t + 1) * LANES] - row_max[c]).astype(BF16)
                 for t in range(slabs)]
            out.append(accs[c] + jnp.dot(jnp.concatenate(p, axis=1), vj, preferred_element_type=F32))
        return tuple(out)

    zero = jnp.zeros((tq, 2 * LANES), F32)
    acc0, acc1 = lax.fori_loop(0, seq // tk, weigh, (zero, zero), unroll=2)
    o = (acc0[:, :HEAD_DIM] / acc0[:, HEAD_DIM:]) - lam * (acc1[:, :HEAD_DIM] / acc1[:, HEAD_DIM:])
    o = o * lax.rsqrt(jnp.mean(o * o, axis=-1, keepdims=True) + DIFF_NORM_EPS)
    o_ref[0] = o * nw_ref[...] * (1.0 - LAMBDA_INIT)


def _diff_attention(zn, zt, batch, seq, lq1, lk1, lq2, lk2, norm_w, tq=512, tk=512):
    zv = zn.reshape(batch, seq, zn.shape[1])
    hpt = BRANCH_WIDTH // HEAD_DIM
    vec = lambda n: pl.BlockSpec((1, n), lambda b, h, qi: (0, 0))
    return pl.pallas_call(
        functools.partial(_diff_kernel, seq=seq, tk=tk),
        scratch_shapes=[pltpu.VMEM((2, seq // tk, tq, tk), F32)],
        out_shape=jax.ShapeDtypeStruct((batch, seq, BRANCH_WIDTH), F32),
        grid=(batch, N_DIFF_HEADS, seq // tq),
        in_specs=[
            vec(HEAD_DIM // 2), vec(HEAD_DIM // 2), vec(HEAD_DIM // 2), vec(HEAD_DIM // 2), vec(HEAD_DIM),
            pl.BlockSpec((1, tq, HEAD_DIM), lambda b, h, qi: (b, qi, zt["qb"] * hpt + h)),
            pl.BlockSpec((1, seq, HEAD_DIM), lambda b, h, qi: (b, 0, zt["kb"] * hpt + h)),
            pl.BlockSpec((1, seq, HEAD_DIM), lambda b, h, qi: (b, 0, zt["vb"] * hpt + h)),
        ],
        out_specs=pl.BlockSpec((1, tq, HEAD_DIM), lambda b, h, qi: (b, qi, h)),
        compiler_params=_cparams(("arbitrary", "arbitrary", "arbitrary")),
        name="diff_attn",
    )(lq1, lk1, lq2, lk2, norm_w, zv, zv, zv).reshape(batch * seq, BRANCH_WIDTH)


def _layer_norm(h, g, b):
    mu = jnp.mean(h, axis=-1, keepdims=True)
    hc = h - mu
    var = jnp.mean(hc * hc, axis=-1, keepdims=True)
    return hc * lax.rsqrt(var + LN_EPS) * g + b


def _post_kernel(o0_ref, o1_ref, o2_ref, l0_ref, l1_ref, l2_ref, ob_ref, g0_ref, g1_ref, x_ref,
                 wb_ref, wo_ref, lng_ref, lnb_ref, wr_ref, x1_ref, yext_ref, *, n_exp):
    d = x_ref.shape[1]
    heads = lambda ref: jnp.concatenate([ref[h] for h in range(HEADS_PER_GROUP)], axis=1)
    l0, l1, l2 = heads(l0_ref), heads(l1_ref), heads(l2_ref)
    lm = jnp.maximum(jnp.maximum(l0, l1), l2)
    e0, e1, e2 = jnp.exp(l0 - lm), jnp.exp(l1 - lm), jnp.exp(l2 - lm)
    den = e0 + e1 + e2
    oa = heads(o0_ref) * (e0 / den) + heads(o1_ref) * (e1 / den) + heads(o2_ref) * (e2 / den)
    bd0 = jnp.dot(oa.astype(BF16), wb_ref[0], preferred_element_type=F32)
    bd1 = jnp.dot(ob_ref[...].astype(BF16), wb_ref[1], preferred_element_type=F32)
    merged = jax.nn.sigmoid(g0_ref[...].astype(F32)) * bd0 + jax.nn.sigmoid(g1_ref[...].astype(F32)) * bd1
    mix = jnp.dot(merged.astype(BF16), wo_ref[...], preferred_element_type=F32)
    x1 = _layer_norm(ALPHA * x_ref[...] + mix, lng_ref[...], lnb_ref[...])
    logits = jnp.dot(x1.astype(BF16), wr_ref[...], preferred_element_type=F32)
    lane = lax.broadcasted_iota(jnp.int32, logits.shape, 1)
    logits = jnp.where(lane < n_exp, logits, -jnp.inf)
    ex = jnp.exp(logits - jnp.max(logits, axis=-1, keepdims=True))
    aff = ex / jnp.sum(ex, axis=-1, keepdims=True)
    x1_ref[...] = x1
    yext_ref[:, :d] = ALPHA * x1
    yext_ref[:, d:] = aff


def _post(o, lse, ob, zn, x2, wb, wo, lng, lnb, wr, n_exp, tm=256):
    t, d = x2.shape
    row = lambda w: pl.BlockSpec((tm, w), lambda i: (i, 0))
    hm = pl.BlockSpec((HEADS_PER_GROUP, tm, HEAD_DIM), lambda i: (0, i, 0))
    full = lambda a: pl.BlockSpec(a.shape, lambda i: (0,) * a.ndim)
    return pl.pallas_call(
        functools.partial(_post_kernel, n_exp=n_exp),
        out_shape=(jax.ShapeDtypeStruct((t, d), F32), jax.ShapeDtypeStruct((t, d + AFF_PAD), F32)),
        grid=(t // tm,),
        in_specs=[hm] * 6 + [
            row(BRANCH_WIDTH),
            pl.BlockSpec((tm, d), lambda i: (i, 0)),
            pl.BlockSpec((tm, d), lambda i: (i, 1)),
            row(d), full(wb), full(wo), full(lng), full(lnb), full(wr),
        ],
        out_specs=(row(d), row(d + AFF_PAD)),
        compiler_params=_cparams(("arbitrary",)),
        name="post_attn",
    )(o[0], o[1], o[2], lse[0], lse[1], lse[2], ob, zn, zn, x2, wb, wo, lng, lnb, wr)


def _topk_kernel(aff_ref, idx_ref, eqcs_ref, cs_ref, *, seq, cap, n_exp):
    b = pl.program_id(0)
    aff = aff_ref[...]

    def search(i, t):
        cand = t | jnp.left_shift(jnp.int32(1), 30 - i)
        cnt = jnp.sum((aff >= pltpu.bitcast(cand, F32)).astype(F32), axis=0, keepdims=True)
        return jnp.where(cnt >= cap, cand, t)

    thr = pltpu.bitcast(lax.fori_loop(0, 31, search, jnp.zeros((1, LANES), jnp.int32)), F32)
    need = cap - jnp.sum((aff > thr).astype(F32), axis=0, keepdims=True)

    tri = (lax.broadcasted_iota(jnp.int32, (LANES, LANES), 1)
           <= lax.broadcasted_iota(jnp.int32, (LANES, LANES), 0)).astype(BF16)

    def cumsum_rows(mask_fn, out_ref):
        def blk(i, off):
            r0 = pl.multiple_of(i * LANES, LANES)
            part = jnp.dot(tri, mask_fn(r0).astype(BF16), preferred_element_type=F32) + off
            out_ref[pl.ds(r0, LANES), :] = part
            return part[LANES - 1:LANES, :]
        lax.fori_loop(0, seq // LANES, blk, jnp.zeros((1, LANES), F32))

    cumsum_rows(lambda r0: aff_ref[pl.ds(r0, LANES), :] == thr, eqcs_ref)

    def selected(r0):
        a = aff_ref[pl.ds(r0, LANES), :]
        return (a > thr) | ((a == thr) & (eqcs_ref[pl.ds(r0, LANES), :] <= need))

    cumsum_rows(selected, cs_ref)

    chunk = 512
    slot = lax.broadcasted_iota(jnp.int32, (1, cap), 1).astype(F32)
    lane = lax.broadcasted_iota(jnp.int32, (chunk, LANES), 1)
    rows = []
    for e in range(n_exp):
        def count(i, tot, e=e):
            r0 = pl.multiple_of(i * chunk, chunk)
            col = jnp.sum(jnp.where(lane == e, cs_ref[pl.ds(r0, chunk), :], 0.0), axis=1, keepdims=True)
            return tot + jnp.sum((col <= slot).astype(F32), axis=0, keepdims=True)
        rows.append(lax.fori_loop(0, seq // chunk, count, jnp.zeros((1, cap), F32)))
    idx_ref[0] = jnp.concatenate(rows, axis=0).astype(jnp.int32) + b * seq


def _topk(yext, batch, seq, d, n_exp, cap):
    return pl.pallas_call(
        functools.partial(_topk_kernel, seq=seq, cap=cap, n_exp=n_exp),
        out_shape=jax.ShapeDtypeStruct((batch, n_exp, cap), jnp.int32),
        grid=(batch,),
        in_specs=[pl.BlockSpec((seq, AFF_PAD), lambda b: (b, d // AFF_PAD))],
        out_specs=pl.BlockSpec((1, n_exp, cap), lambda b: (b, 0, 0)),
        scratch_shapes=[pltpu.VMEM((seq, LANES), F32), pltpu.VMEM((seq, LANES), F32)],
        compiler_params=_cparams(("arbitrary",)),
        name="expert_topk",
    )(yext)


def _ffn_kernel(idx_ref, x1_hbm, yin_hbm, wg_ref, wu_ref, wd_ref, yext_hbm, xg_ref, xb_ref, yg_ref, acc_ref, sems,
                *, rows, d, n_split, nf):
    del yin_hbm
    e, hf, f = pl.program_id(0), pl.program_id(1), pl.program_id(2)
    n_groups = pl.num_programs(0) * n_split
    grp = e * n_split + hf
    base = grp * rows
    nxt = jnp.minimum(grp + 1, n_groups - 1) * rows
    prev = jnp.where(grp > 0, base - rows, rows)
    per = rows // nf
    yg_cur, yg_prev = yg_ref.at[grp % 2], yg_ref.at[(grp + 1) % 2]

    def x_copy(r, tok):
        return pltpu.make_async_copy(x1_hbm.at[pl.ds(tok, 1)], xg_ref.at[pl.ds(r, 1)], sems.at[0])

    def y_in_copy(buf, r, tok):
        return pltpu.make_async_copy(yext_hbm.at[pl.ds(tok, 1)], buf.at[pl.ds(r, 1)], sems.at[1])

    def y_out_copy(buf, r, tok):
        return pltpu.make_async_copy(buf.at[pl.ds(r, 1)], yext_hbm.at[pl.ds(tok, 1)], sems.at[2])

    def for_rows(fn, start):
        def body(r, c):
            fn(r, idx_ref[start + r])
            return c
        lax.fori_loop(0, rows, body, 0, unroll=8)

    @pl.when((f == 0) & (grp == 0))
    def _():
        for_rows(lambda r, tok: x_copy(r, tok).start(), base)
        for_rows(lambda r, tok: y_in_copy(yg_prev, r, tok).start(), prev)
        for_rows(lambda r, tok: y_in_copy(yg_prev, r, tok).wait(), prev)

    @pl.when(f == 0)
    def _():
        for_rows(lambda r, tok: x_copy(r, tok).wait(), base)
        xb_ref[...] = xg_ref[...].astype(BF16)
        acc_ref[...] = jnp.zeros_like(acc_ref)

    for r in range(per):
        row = f * per + r
        x_copy(row, idx_ref[nxt + row]).start()
        y_in_copy(yg_cur, row, idx_ref[base + row]).start()
        y_out_copy(yg_prev, row, idx_ref[prev + row]).start()

    wg, wu, wd = wg_ref[0].astype(BF16), wu_ref[0].astype(BF16), wd_ref[0].astype(BF16)
    sub = rows // 2
    for rs in (pl.ds(0, sub), pl.ds(sub, sub)):
        x = xb_ref[rs, :]
        g = jnp.dot(x, wg, preferred_element_type=F32)
        u = jnp.dot(x, wu, preferred_element_type=F32)
        hid = (g * jax.nn.sigmoid(g)) * u
        acc_ref[rs, :] += jnp.dot(hid.astype(BF16), wd, preferred_element_type=F32)

    @pl.when(f == nf - 1)
    def _():
        for_rows(lambda r, tok: y_in_copy(yg_cur, r, tok).wait(), base)
        for_rows(lambda r, tok: y_out_copy(yg_prev, r, tok).wait(), prev)
        aff = yg_cur[:, d:]
        lane = lax.broadcasted_iota(jnp.int32, aff.shape, 1)
        gate = jnp.sum(jnp.where(lane == e, aff, 0.0), axis=-1, keepdims=True)
        yg_cur[:, :d] = yg_cur[:, :d] + acc_ref[...] * gate

    @pl.when((f == nf - 1) & (grp == n_groups - 1))
    def _():
        for_rows(lambda r, tok: x_copy(r, tok).wait(), base)
        for_rows(lambda r, tok: y_out_copy(yg_cur, r, tok).start(), base)
        for_rows(lambda r, tok: y_out_copy(yg_cur, r, tok).wait(), base)


def _expert_ffn(idx_flat, x1, yext, w_gate, w_up, w_down, cap, tf=256, n_split=2):
    t, d = x1.shape
    n_exp, _, ff = w_gate.shape
    tf = min(tf, ff)
    nf = ff // tf
    rows = idx_flat.shape[0] // (n_exp * n_split)
    assert rows % nf == 0 and n_split == 2 and rows % cap == 0
    grid_spec = pltpu.PrefetchScalarGridSpec(
        num_scalar_prefetch=1,
        grid=(n_exp, n_split, nf),
        in_specs=[
            pl.BlockSpec(memory_space=pl.ANY),
            pl.BlockSpec(memory_space=pl.ANY),
            pl.BlockSpec((1, d, tf), lambda e, h, f, idx: (e, 0, f)),
            pl.BlockSpec((1, d, tf), lambda e, h, f, idx: (e, 0, f)),
            pl.BlockSpec((1, tf, d), lambda e, h, f, idx: (e, f, 0)),
        ],
        out_specs=pl.BlockSpec(memory_space=pl.ANY),
        scratch_shapes=[
            pltpu.VMEM((rows, d), F32),
            pltpu.VMEM((rows, d), BF16),
            pltpu.VMEM((2, rows, d + AFF_PAD), F32),
            pltpu.VMEM((rows, d), F32),
            pltpu.SemaphoreType.DMA((3,)),
        ],
    )
    return pl.pallas_call(
        functools.partial(_ffn_kernel, rows=rows, d=d, n_split=n_split, nf=nf),
        out_shape=jax.ShapeDtypeStruct(yext.shape, yext.dtype),
        grid_spec=grid_spec,
        input_output_aliases={2: 0},
        compiler_params=_cparams(("arbitrary", "arbitrary", "arbitrary")),
        name="expert_ffn",
    )(idx_flat, x1, yext, w_gate, w_up, w_down)


def _ln2_kernel(y_ref, g_ref, b_ref, o_ref):
    o_ref[...] = _layer_norm(y_ref[...], g_ref[...], b_ref[...])


def _ln2(yext, d, g, b, tm=512):
    t = yext.shape[0]
    return pl.pallas_call(
        _ln2_kernel,
        out_shape=jax.ShapeDtypeStruct((t, d), F32),
        grid=(t // tm,),
        in_specs=[pl.BlockSpec((tm, d), lambda i: (i, 0)),
                  pl.BlockSpec((1, d), lambda i: (0, 0)), pl.BlockSpec((1, d), lambda i: (0, 0))],
        out_specs=pl.BlockSpec((tm, d), lambda i: (i, 0)),
        compiler_params=_cparams(("arbitrary",)),
        name="ln2",
    )(yext, g, b)


def kernel(x, w_in, lambda_q1, lambda_k1, lambda_q2, lambda_k2, diff_norm_w, w_branch, w_out, ln1_g, ln1_b,
           w_router, w_gate, w_up, w_down, ln2_g, ln2_b):
    batch, seq, d = x.shape
    assert w_in.shape[0] == 1, "one layer"
    n_exp = w_router.shape[-1]
    cap = EC_CAPACITY * seq // n_exp
    x2 = x.reshape(batch * seq, d)

    zn, zd1, zd2 = _in_proj(x2, w_in[0], batch, seq)
    zt = _zn_tile(zn.shape[1] // BRANCH_WIDTH - 6)
    zn4 = zn.reshape(batch, 1, seq, zn.shape[1])
    o, lse = zip(_band_attention(zn4, (zt["qa"], zt["ka"], zt["va"]), batch, seq, 0),
                 _band_attention(zd1, (0, 1, 2), batch, seq, 1),
                 _band_attention(zd2, (0, 1, 2), batch, seq, 2))
    ob = _diff_attention(zn, zt, batch, seq, lambda_q1, lambda_k1, lambda_q2, lambda_k2, diff_norm_w)

    wr = jnp.pad(w_router[0], ((0, 0), (0, AFF_PAD - n_exp))).astype(BF16)
    x1, yext = _post(o, lse, ob, zn, x2, w_branch[0].astype(BF16), w_out[0].astype(BF16),
                     ln1_g, ln1_b, wr, n_exp)
    idx = _topk(yext, batch, seq, d, n_exp, cap)
    idx_flat = idx.transpose(1, 0, 2).reshape(-1)
    yext = _expert_ffn(idx_flat, x1, yext, w_gate[0], w_up[0], w_down[0], cap)
    return _ln2(yext, d, ln2_g, ln2_b).reshape(batch, seq, d)
```

```python
import functools
import math

import jax
import jax.numpy as jnp
from jax import lax
from jax.experimental import pallas as pl
from jax.experimental.pallas import tpu as pltpu

F32 = jnp.float32
BF16 = jnp.bfloat16

HEAD_DIM = 128
DIL_GROUPS = ((128, 1), (512, 4), (2048, 16))
HEADS_PER_GROUP = 4
N_GROUPS = len(DIL_GROUPS)
N_DIFF_HEADS = 4
BRANCH_WIDTH = HEADS_PER_GROUP * HEAD_DIM
ROPE_THETA = 500000.0
ROPE_FRACTION = 4
EC_CAPACITY = 2
LN_EPS = 1e-5
DIFF_NORM_EPS = 1e-5
NEG_INF = -1e30
LAMBDA_INIT = 0.8 - 0.6 * math.exp(-0.3 * 0)
ALPHA = 2.0 ** 0.25
DIFF_Q_SCALE = (HEAD_DIM // 2) ** -0.5 * math.log2(math.e)

N_DIL_TILES = 3 * N_GROUPS
QB_TILE, KB_TILE, VB_TILE = N_DIL_TILES, N_DIL_TILES + 1, N_DIL_TILES + 2
GATE_TILE = N_DIL_TILES + 3
LANES = 128
AFF_PAD = LANES
BAND_TQ = 128
VMEM_LIMIT = 56 * 1024 * 1024


def _cparams(sem):
    return pltpu.CompilerParams(dimension_semantics=sem, vmem_limit_bytes=VMEM_LIMIT)


def _rope_tables(seq, dh):
    rot = dh // ROPE_FRACTION
    half = rot // 2
    inv_freq = ROPE_THETA ** (-2.0 * jnp.arange(half, dtype=F32) / rot)
    ang = jnp.arange(seq).astype(F32)[:, None] * inv_freq[None, :]
    cos, sin = jnp.cos(ang), jnp.sin(ang)
    one = jnp.ones((seq, dh - rot), F32)
    zero = lambda n: jnp.zeros((seq, n), F32)
    c = jnp.concatenate([cos, cos, one], axis=-1)
    s_neg = jnp.concatenate([-sin, zero(dh - half)], axis=-1)
    s_pos = jnp.concatenate([zero(half), sin, zero(dh - rot)], axis=-1)
    tab = jnp.stack([c, s_neg, s_pos])
    return jnp.tile(tab, (1, 1, LANES // dh)), half


def _rope_slab(a, tab_ref, half):
    return (a * tab_ref[0] + pltpu.roll(a, LANES - half, 1) * tab_ref[1]
            + pltpu.roll(a, half, 1) * tab_ref[2])


def _slabs(acc):
    return [acc[:, h * LANES:(h + 1) * LANES] for h in range(acc.shape[1] // LANES)]


def _in_proj_kernel(x_ref, w_ref, ra_ref, rb_ref, zn_ref, zd1_ref, zd2_ref, xb_ref, slab_ref, *, half_a, half_b):
    j = pl.program_id(1)

    @pl.when(j == 0)
    def _():
        xb_ref[...] = x_ref[...].astype(BF16)

    is_dil = j < N_DIL_TILES
    grp = j % N_GROUPS
    dil_rope = j < 2 * N_GROUPS
    is_b = jnp.logical_or(j == QB_TILE, j == KB_TILE)

    def project():
        return jnp.dot(xb_ref[...], w_ref[...], preferred_element_type=F32)

    def rope_a(do_rope):
        return [_rope_slab(a, ra_ref, half_a) if do_rope else a for a in _slabs(project())]

    def to_natural(slabs):
        zn_ref[...] = jnp.concatenate(slabs, axis=1).astype(BF16)

    def to_phase_major(slabs, out_ref, r):
        for h, s in enumerate(slabs):
            slab_ref[h] = s
        rows = slab_ref.shape[1] // r
        for p in range(r):
            out_ref[0, p] = jnp.concatenate(
                [slab_ref[h, pl.ds(p, rows, stride=r), :] for h in range(len(slabs))], axis=1).astype(BF16)

    @pl.when(is_dil & dil_rope & (grp == 0))
    def _():
        to_natural(rope_a(True))

    for do_rope, cond in ((True, dil_rope), (False, jnp.logical_not(dil_rope))):
        @pl.when(is_dil & cond & (grp == 1))
        def _(do_rope=do_rope):
            to_phase_major(rope_a(do_rope), zd1_ref, DIL_GROUPS[1][1])

        @pl.when(is_dil & cond & (grp == 2))
        def _(do_rope=do_rope):
            to_phase_major(rope_a(do_rope), zd2_ref, DIL_GROUPS[2][1])

    @pl.when(is_b)
    def _():
        scale = jnp.where(j == QB_TILE, DIFF_Q_SCALE, 1.0).astype(F32)
        to_natural([_rope_slab(a, rb_ref, half_b) * scale for a in _slabs(project())])

    @pl.when(jnp.logical_not(is_b) & jnp.logical_not(is_dil & (dil_rope | (grp != 0))))
    def _():
        zn_ref[...] = project().astype(BF16)


def _zn_tile(n_gate_tiles):
    return dict(qa=n_gate_tiles, ka=n_gate_tiles + 1, va=n_gate_tiles + 2,
                qb=n_gate_tiles + 3, kb=n_gate_tiles + 4, vb=n_gate_tiles + 5)


def _in_proj(x2, w, batch, seq, tm=1024):
    t, d = x2.shape
    n_tiles = w.shape[1] // BRANCH_WIDTH
    n_gate_tiles = n_tiles - GATE_TILE
    tn = BRANCH_WIDTH
    tm = min(tm, seq)
    spt = seq // tm
    ra, half_a = _rope_tables(seq, HEAD_DIM)
    rb, half_b = _rope_tables(seq, HEAD_DIM // 2)
    r1, r2 = DIL_GROUPS[1][1], DIL_GROUPS[2][1]
    zt = _zn_tile(n_gate_tiles)

    def zn_col(j):
        return jnp.where(j >= GATE_TILE, j - GATE_TILE,
                         jnp.where(j >= N_DIL_TILES, j - N_DIL_TILES + zt["qb"], zt["qa"] + j // N_GROUPS))

    def zd_col(g):
        return lambda j: jnp.clip((j - g) // N_GROUPS, 0, 2)

    return pl.pallas_call(
        functools.partial(_in_proj_kernel, half_a=half_a, half_b=half_b),
        out_shape=(
            jax.ShapeDtypeStruct((t, (n_gate_tiles + 6) * tn), BF16),
            jax.ShapeDtypeStruct((batch, r1, seq // r1, 3 * tn), BF16),
            jax.ShapeDtypeStruct((batch, r2, seq // r2, 3 * tn), BF16),
        ),
        grid=(t // tm, n_tiles),
        in_specs=[
            pl.BlockSpec((tm, d), lambda i, j: (i, 0)),
            pl.BlockSpec((d, tn), lambda i, j: (0, j)),
            pl.BlockSpec((3, tm, LANES), lambda i, j: (0, i % spt, 0)),
            pl.BlockSpec((3, tm, LANES), lambda i, j: (0, i % spt, 0)),
        ],
        out_specs=(
            pl.BlockSpec((tm, tn), lambda i, j: (i, zn_col(j))),
            pl.BlockSpec((1, r1, tm // r1, tn), lambda i, j: (i // spt, 0, i % spt, zd_col(1)(j))),
            pl.BlockSpec((1, r2, tm // r2, tn), lambda i, j: (i // spt, 0, i % spt, zd_col(2)(j))),
        ),
        scratch_shapes=[pltpu.VMEM((tm, d), BF16), pltpu.VMEM((tn // LANES, tm, LANES), F32)],
        compiler_params=_cparams(("arbitrary", "arbitrary")),
        name="in_proj",
    )(x2, w, ra, rb)


def _band_kernel(q_ref, k_ref, v_ref, o_ref, lse_ref, *, seq_l, tl, r, half, scale):
    li = pl.program_id(1)
    win = BAND_TQ + 2 * half
    row = lax.broadcasted_iota(jnp.int32, (BAND_TQ, win), 0)
    col = lax.broadcasted_iota(jnp.int32, (BAND_TQ, win), 1)
    tiles = tl // BAND_TQ

    def body(it, carry):
        p = it // tiles
        t = it % tiles
        r0 = pl.multiple_of(t * BAND_TQ, BAND_TQ)
        q0 = li * tl + r0
        ws = pl.multiple_of(jnp.clip(q0 - half, 0, seq_l - win), half)
        valid = jnp.abs((q0 + row) - (ws + col)) <= half
        for h in range(HEADS_PER_GROUP):
            hs = slice(h * HEAD_DIM, (h + 1) * HEAD_DIM)
            q = q_ref[0, p, pl.ds(r0, BAND_TQ), hs]
            k = k_ref[0, p, pl.ds(ws, win), hs]
            v = v_ref[0, p, pl.ds(ws, win), hs]
            s = lax.dot_general(q, k, (((1,), (1,)), ((), ())), preferred_element_type=F32) * scale
            s = jnp.where(valid, s, NEG_INF)
            m = jnp.max(s, axis=-1, keepdims=True)
            pr = jnp.exp(s - m)
            den = jnp.sum(pr, axis=-1, keepdims=True)
            acc = jnp.dot(pr.astype(BF16), v, preferred_element_type=F32)
            dst = pl.ds(r0 * r + p, BAND_TQ, stride=r) if r > 1 else pl.ds(r0, BAND_TQ)
            o_ref[h, dst, :] = acc / den
            lse_ref[h, dst, :] = jnp.broadcast_to(m + jnp.log(den), (BAND_TQ, HEAD_DIM))
        return carry

    lax.fori_loop(0, r * tiles, body, 0, unroll=4)


def _band_attention(zsrc, tiles_qkv, batch, seq, g):
    window, r = DIL_GROUPS[g]
    half = window // (2 * r)
    seq_l = seq // r
    tl = max(BAND_TQ, 512 // r)
    tq_, tk_, tv_ = tiles_qkv
    t = batch * seq
    nl = seq_l // tl
    out_sd = jax.ShapeDtypeStruct((HEADS_PER_GROUP, t, HEAD_DIM), F32)
    o_spec = pl.BlockSpec((HEADS_PER_GROUP, tl * r, HEAD_DIM), lambda b, li: (0, b * nl + li, 0))
    return pl.pallas_call(
        functools.partial(_band_kernel, seq_l=seq_l, tl=tl, r=r, half=half, scale=HEAD_DIM ** -0.5),
        out_shape=(out_sd, out_sd),
        grid=(batch, nl),
        in_specs=[
            pl.BlockSpec((1, r, tl, BRANCH_WIDTH), lambda b, li: (b, 0, li, tq_)),
            pl.BlockSpec((1, r, seq_l, BRANCH_WIDTH), lambda b, li: (b, 0, 0, tk_)),
            pl.BlockSpec((1, r, seq_l, BRANCH_WIDTH), lambda b, li: (b, 0, 0, tv_)),
        ],
        out_specs=(o_spec, o_spec),
        compiler_params=_cparams(("arbitrary", "arbitrary")),
        name=f"band_attn_g{g}",
    )(zsrc, zsrc, zsrc)


def _diff_kernel(lq1_ref, lk1_ref, lq2_ref, lk2_ref, nw_ref, q_ref, k_ref, v_ref, o_ref, s_ref, *, seq, tk):
    lam = (jnp.exp(jnp.sum(lq1_ref[...] * lk1_ref[...])) - jnp.exp(jnp.sum(lq2_ref[...] * lk2_ref[...]))
           + LAMBDA_INIT)
    dc = HEAD_DIM // 2
    q = q_ref[0]
    tq = q.shape[0]
    lane = lax.broadcasted_iota(jnp.int32, (tq, HEAD_DIM), 1)
    qc = [jnp.where(lane < dc, q, jnp.zeros_like(q)), jnp.where(lane >= dc, q, jnp.zeros_like(q))]
    slabs = tk // LANES

    def scores(j, mx):
        kj = k_ref[0, pl.ds(pl.multiple_of(j * tk, tk), tk), :]
        out = []
        for c in range(2):
            s = lax.dot_general(qc[c], kj, (((1,), (1,)), ((), ())), preferred_element_type=F32)
            s_ref[c, j] = s
            m = mx[c]
            for t in range(slabs):
                m = jnp.maximum(m, s[:, t * LANES:(t + 1) * LANES])
            out.append(m)
        return tuple(out)

    neg = jnp.full((tq, LANES), -jnp.inf, F32)
    mx = lax.fori_loop(0, seq // tk, scores, (neg, neg), unroll=2)
    row_max = [jnp.broadcast_to(jnp.max(m, axis=-1, keepdims=True), (tq, LANES)) for m in mx]

    ones = jnp.ones((tk, LANES), BF16)

    def weigh(j, accs):
        vj = jnp.concatenate([v_ref[0, pl.ds(pl.multiple_of(j * tk, tk), tk), :], ones], axis=1)
        out = []
        for c in range(2):
            p = [jnp.exp2(s_ref[c, j, :, t * LANES:(t + 1) * LANES] - row_max[c]).astype(BF16)
                 for t in range(slabs)]
            out.append(accs[c] + jnp.dot(jnp.concatenate(p, axis=1), vj, preferred_element_type=F32))
        return tuple(out)

    zero = jnp.zeros((tq, 2 * LANES), F32)
    acc0, acc1 = lax.fori_loop(0, seq // tk, weigh, (zero, zero), unroll=2)
    o = (acc0[:, :HEAD_DIM] / acc0[:, HEAD_DIM:]) - lam * (acc1[:, :HEAD_DIM] / acc1[:, HEAD_DIM:])
    o = o * lax.rsqrt(jnp.mean(o * o, axis=-1, keepdims=True) + DIFF_NORM_EPS)
    o_ref[0] = o * nw_ref[...] * (1.0 - LAMBDA_INIT)


def _diff_attention(zn, zt, batch, seq, lq1, lk1, lq2, lk2, norm_w, tq=1024, tk=2048):
    zv = zn.reshape(batch, seq, zn.shape[1])
    hpt = BRANCH_WIDTH // HEAD_DIM
    vec = lambda n: pl.BlockSpec((1, n), lambda b, h, qi: (0, 0))
    return pl.pallas_call(
        functools.partial(_diff_kernel, seq=seq, tk=tk),
        scratch_shapes=[pltpu.VMEM((2, seq // tk, tq, tk), F32)],
        out_shape=jax.ShapeDtypeStruct((batch, seq, BRANCH_WIDTH), F32),
        grid=(batch, N_DIFF_HEADS, seq // tq),
        in_specs=[
            vec(HEAD_DIM // 2), vec(HEAD_DIM // 2), vec(HEAD_DIM // 2), vec(HEAD_DIM // 2), vec(HEAD_DIM),
            pl.BlockSpec((1, tq, HEAD_DIM), lambda b, h, qi: (b, qi, zt["qb"] * hpt + h)),
            pl.BlockSpec((1, seq, HEAD_DIM), lambda b, h, qi: (b, 0, zt["kb"] * hpt + h)),
            pl.BlockSpec((1, seq, HEAD_DIM), lambda b, h, qi: (b, 0, zt["vb"] * hpt + h)),
        ],
        out_specs=pl.BlockSpec((1, tq, HEAD_DIM), lambda b, h, qi: (b, qi, h)),
        compiler_params=_cparams(("arbitrary", "arbitrary", "arbitrary")),
        name="diff_attn",
    )(lq1, lk1, lq2, lk2, norm_w, zv, zv, zv).reshape(batch * seq, BRANCH_WIDTH)


def _layer_norm(h, g, b):
    mu = jnp.mean(h, axis=-1, keepdims=True)
    hc = h - mu
    var = jnp.mean(hc * hc, axis=-1, keepdims=True)
    return hc * lax.rsqrt(var + LN_EPS) * g + b


def _post_kernel(o0_ref, o1_ref, o2_ref, l0_ref, l1_ref, l2_ref, ob_ref, g0_ref, g1_ref, x_ref,
                 wb_ref, wo_ref, lng_ref, lnb_ref, wr_ref, x1_ref, yext_ref, *, n_exp):
    d = x_ref.shape[1]
    heads = lambda ref: jnp.concatenate([ref[h] for h in range(HEADS_PER_GROUP)], axis=1)
    l0, l1, l2 = heads(l0_ref), heads(l1_ref), heads(l2_ref)
    lm = jnp.maximum(jnp.maximum(l0, l1), l2)
    e0, e1, e2 = jnp.exp(l0 - lm), jnp.exp(l1 - lm), jnp.exp(l2 - lm)
    den = e0 + e1 + e2
    oa = heads(o0_ref) * (e0 / den) + heads(o1_ref) * (e1 / den) + heads(o2_ref) * (e2 / den)
    bd0 = jnp.dot(oa.astype(BF16), wb_ref[0], preferred_element_type=F32)
    bd1 = jnp.dot(ob_ref[...].astype(BF16), wb_ref[1], preferred_element_type=F32)
    merged = jax.nn.sigmoid(g0_ref[...].astype(F32)) * bd0 + jax.nn.sigmoid(g1_ref[...].astype(F32)) * bd1
    mix = jnp.dot(merged.astype(BF16), wo_ref[...], preferred_element_type=F32)
    x1 = _layer_norm(ALPHA * x_ref[...] + mix, lng_ref[...], lnb_ref[...])
    logits = jnp.dot(x1.astype(BF16), wr_ref[...], preferred_element_type=F32)
    lane = lax.broadcasted_iota(jnp.int32, logits.shape, 1)
    logits = jnp.where(lane < n_exp, logits, -jnp.inf)
    ex = jnp.exp(logits - jnp.max(logits, axis=-1, keepdims=True))
    aff = ex / jnp.sum(ex, axis=-1, keepdims=True)
    x1_ref[...] = x1
    yext_ref[:, :d] = ALPHA * x1
    yext_ref[:, d:] = aff


def _post(o, lse, ob, zn, x2, wb, wo, lng, lnb, wr, n_exp, tm=256):
    t, d = x2.shape
    row = lambda w: pl.BlockSpec((tm, w), lambda i: (i, 0))
    hm = pl.BlockSpec((HEADS_PER_GROUP, tm, HEAD_DIM), lambda i: (0, i, 0))
    full = lambda a: pl.BlockSpec(a.shape, lambda i: (0,) * a.ndim)
    return pl.pallas_call(
        functools.partial(_post_kernel, n_exp=n_exp),
        out_shape=(jax.ShapeDtypeStruct((t, d), F32), jax.ShapeDtypeStruct((t, d + AFF_PAD), F32)),
        grid=(t // tm,),
        in_specs=[hm] * 6 + [
            row(BRANCH_WIDTH),
            pl.BlockSpec((tm, d), lambda i: (i, 0)),
            pl.BlockSpec((tm, d), lambda i: (i, 1)),
            row(d), full(wb), full(wo), full(lng), full(lnb), full(wr),
        ],
        out_specs=(row(d), row(d + AFF_PAD)),
        compiler_params=_cparams(("arbitrary",)),
        name="post_attn",
    )(o[0], o[1], o[2], lse[0], lse[1], lse[2], ob, zn, zn, x2, wb, wo, lng, lnb, wr)


def _topk_kernel(aff_ref, idx_ref, loc_ref, *, seq, cap, n_exp):
    b = pl.program_id(0)
    nb = seq // LANES
    blocks = [aff_ref[k * LANES:(k + 1) * LANES, :].T[:n_exp, :] for k in range(nb)]
    lane = lax.broadcasted_iota(jnp.int32, (n_exp, LANES), 1)
    i0 = lax.broadcasted_iota(jnp.int32, (LANES, LANES), 0)
    i1 = lax.broadcasted_iota(jnp.int32, (LANES, LANES), 1)
    incl = (i0 <= i1).astype(BF16)
    excl = (i0 < i1).astype(BF16)

    def count(pred):
        acc = jnp.zeros((n_exp, LANES), F32)
        for blk in blocks:
            acc = acc + pred(blk).astype(F32)
        return jnp.sum(acc, axis=1, keepdims=True)

    def search(i, t):
        cand = t | jnp.left_shift(jnp.int32(1), 30 - i)
        cf = pltpu.bitcast(cand, F32)
        return jnp.where(count(lambda blk: blk >= cf) >= cap, cand, t)

    thr = pltpu.bitcast(lax.fori_loop(0, 31, search, jnp.zeros((n_exp, LANES), jnp.int32)), F32)
    need = cap - count(lambda blk: blk > thr)

    def running(masks):
        local = [jnp.dot(m.astype(BF16), incl, preferred_element_type=F32) for m in masks]
        tot = jnp.zeros((n_exp, LANES), F32)
        for k, part in enumerate(local):
            tot = jnp.where(lane == k, part[:, LANES - 1:LANES], tot)
        return local, tot, jnp.dot(tot.astype(BF16), excl, preferred_element_type=F32)

    eq_local, _, eq_off = running([blk == thr for blk in blocks])
    sel = [(blk > thr) | ((blk == thr) & (eq_local[k] + eq_off[:, k:k + 1] <= need))
           for k, blk in enumerate(blocks)]
    sel_local, sel_tot, sel_off = running(sel)
    for k, part in enumerate(sel_local):
        loc_ref[k * n_exp:(k + 1) * n_exp, :] = part
    cum_end = sel_off + sel_tot

    slot = lax.broadcasted_iota(jnp.int32, (cap, 1), 0).astype(F32)
    lane_c = lax.broadcasted_iota(jnp.int32, (cap, LANES), 1)
    lane_f = lane_c.astype(F32)
    pad = jnp.zeros((LANES - nb, LANES), F32)
    res = jnp.zeros((cap, LANES), F32)
    for e in range(n_exp):
        blk_of = jnp.sum((cum_end[e:e + 1, :] <= slot).astype(F32), axis=1, keepdims=True)
        onehot = lane_f == blk_of
        before = jnp.sum(jnp.where(onehot, sel_off[e:e + 1, :], 0.0), axis=1, keepdims=True)
        table = jnp.concatenate([loc_ref[pl.ds(e, nb, stride=n_exp), :], pad], axis=0).astype(BF16)
        local = jnp.dot(onehot.astype(BF16), table, preferred_element_type=F32)
        pos = jnp.sum((local <= slot - before).astype(F32), axis=1, keepdims=True)
        res = jnp.where(lane_c == e, blk_of * LANES + pos, res)
    idx_ref[0] = res.T[:n_exp, :].astype(jnp.int32) + b * seq


def _topk(yext, batch, seq, d, n_exp, cap):
    assert n_exp % 8 == 0 and seq // LANES <= LANES and cap % LANES == 0
    return pl.pallas_call(
        functools.partial(_topk_kernel, seq=seq, cap=cap, n_exp=n_exp),
        out_shape=jax.ShapeDtypeStruct((batch, n_exp, cap), jnp.int32),
        grid=(batch,),
        in_specs=[pl.BlockSpec((seq, AFF_PAD), lambda b: (b, d // AFF_PAD))],
        out_specs=pl.BlockSpec((1, n_exp, cap), lambda b: (b, 0, 0)),
        scratch_shapes=[pltpu.VMEM((seq // LANES * n_exp, LANES), F32)],
        compiler_params=_cparams(("arbitrary",)),
        name="expert_topk",
    )(yext)


def _ffn_kernel(idx_ref, x1_hbm, yin_hbm, wg_ref, wu_ref, wd_ref, yext_hbm, xg_ref, xb_ref, yg_ref, acc_ref, sems,
                *, rows, d, n_split, nf):
    del yin_hbm
    e, hf, f = pl.program_id(0), pl.program_id(1), pl.program_id(2)
    n_groups = pl.num_programs(0) * n_split
    grp = e * n_split + hf
    base = grp * rows
    nxt = jnp.minimum(grp + 1, n_groups - 1) * rows
    prev = jnp.where(grp > 0, base - rows, rows)
    per = rows // nf
    yg_cur, yg_prev = yg_ref.at[grp % 2], yg_ref.at[(grp + 1) % 2]

    def x_copy(r, tok):
        return pltpu.make_async_copy(x1_hbm.at[pl.ds(tok, 1)], xg_ref.at[pl.ds(r, 1)], sems.at[0])

    def y_in_copy(buf, r, tok):
        return pltpu.make_async_copy(yext_hbm.at[pl.ds(tok, 1)], buf.at[pl.ds(r, 1)], sems.at[1])

    def y_out_copy(buf, r, tok):
        return pltpu.make_async_copy(buf.at[pl.ds(r, 1)], yext_hbm.at[pl.ds(tok, 1)], sems.at[2])

    def for_rows(fn, start):
        def body(r, c):
            fn(r, idx_ref[start + r])
            return c
        lax.fori_loop(0, rows, body, 0, unroll=8)

    @pl.when((f == 0) & (grp == 0))
    def _():
        for_rows(lambda r, tok: x_copy(r, tok).start(), base)
        for_rows(lambda r, tok: y_in_copy(yg_prev, r, tok).start(), prev)
        for_rows(lambda r, tok: y_in_copy(yg_prev, r, tok).wait(), prev)

    @pl.when(f == 0)
    def _():
        for_rows(lambda r, tok: x_copy(r, tok).wait(), base)
        xb_ref[...] = xg_ref[...].astype(BF16)
        acc_ref[...] = jnp.zeros_like(acc_ref)

    for r in range(per):
        row = f * per + r
        x_copy(row, idx_ref[nxt + row]).start()
        y_in_copy(yg_cur, row, idx_ref[base + row]).start()
        y_out_copy(yg_prev, row, idx_ref[prev + row]).start()

    wg, wu, wd = wg_ref[0].astype(BF16), wu_ref[0].astype(BF16), wd_ref[0].astype(BF16)
    sub = rows // 2
    for rs in (pl.ds(0, sub), pl.ds(sub, sub)):
        x = xb_ref[rs, :]
        g = jnp.dot(x, wg, preferred_element_type=F32)
        u = jnp.dot(x, wu, preferred_element_type=F32)
        hid = (g * jax.nn.sigmoid(g)) * u
        acc_ref[rs, :] += jnp.dot(hid.astype(BF16), wd, preferred_element_type=F32)

    @pl.when(f == nf - 1)
    def _():
        for_rows(lambda r, tok: y_in_copy(yg_cur, r, tok).wait(), base)
        for_rows(lambda r, tok: y_out_copy(yg_prev, r, tok).wait(), prev)
        aff = yg_cur[:, d:]
        lane = lax.broadcasted_iota(jnp.int32, aff.shape, 1)
        gate = jnp.sum(jnp.where(lane == e, aff, 0.0), axis=-1, keepdims=True)
        yg_cur[:, :d] = yg_cur[:, :d] + acc_ref[...] * gate

    @pl.when((f == nf - 1) & (grp == n_groups - 1))
    def _():
        for_rows(lambda r, tok: x_copy(r, tok).wait(), base)
        for_rows(lambda r, tok: y_out_copy(yg_cur, r, tok).start(), base)
        for_rows(lambda r, tok: y_out_copy(yg_cur, r, tok).wait(), base)


def _expert_ffn(idx_flat, x1, yext, w_gate, w_up, w_down, cap, tf=256, n_split=2):
    t, d = x1.shape
    n_exp, _, ff = w_gate.shape
    tf = min(tf, ff)
    nf = ff // tf
    rows = idx_flat.shape[0] // (n_exp * n_split)
    assert rows % nf == 0 and n_split == 2 and rows % cap == 0
    grid_spec = pltpu.PrefetchScalarGridSpec(
        num_scalar_prefetch=1,
        grid=(n_exp, n_split, nf),
        in_specs=[
            pl.BlockSpec(memory_space=pl.ANY),
            pl.BlockSpec(memory_space=pl.ANY),
            pl.BlockSpec((1, d, tf), lambda e, h, f, idx: (e, 0, f)),
            pl.BlockSpec((1, d, tf), lambda e, h, f, idx: (e, 0, f)),
            pl.BlockSpec((1, tf, d), lambda e, h, f, idx: (e, f, 0)),
        ],
        out_specs=pl.BlockSpec(memory_space=pl.ANY),
        scratch_shapes=[
            pltpu.VMEM((rows, d), F32),
            pltpu.VMEM((rows, d), BF16),
            pltpu.VMEM((2, rows, d + AFF_PAD), F32),
            pltpu.VMEM((rows, d), F32),
            pltpu.SemaphoreType.DMA((3,)),
        ],
    )
    return pl.pallas_call(
        functools.partial(_ffn_kernel, rows=rows, d=d, n_split=n_split, nf=nf),
        out_shape=jax.ShapeDtypeStruct(yext.shape, yext.dtype),
        grid_spec=grid_spec,
        input_output_aliases={2: 0},
        compiler_params=_cparams(("arbitrary", "arbitrary", "arbitrary")),
        name="expert_ffn",
    )(idx_flat, x1, yext, w_gate, w_up, w_down)


def _ln2_kernel(y_ref, g_ref, b_ref, o_ref):
    o_ref[...] = _layer_norm(y_ref[...], g_ref[...], b_ref[...])


def _ln2(yext, d, g, b, tm=512):
    t = yext.shape[0]
    return pl.pallas_call(
        _ln2_kernel,
        out_shape=jax.ShapeDtypeStruct((t, d), F32),
        grid=(t // tm,),
        in_specs=[pl.BlockSpec((tm, d), lambda i: (i, 0)),
                  pl.BlockSpec((1, d), lambda i: (0, 0)), pl.BlockSpec((1, d), lambda i: (0, 0))],
        out_specs=pl.BlockSpec((tm, d), lambda i: (i, 0)),
        compiler_params=_cparams(("arbitrary",)),
        name="ln2",
    )(yext, g, b)


def kernel(x, w_in, lambda_q1, lambda_k1, lambda_q2, lambda_k2, diff_norm_w, w_branch, w_out, ln1_g, ln1_b,
           w_router, w_gate, w_up, w_down, ln2_g, ln2_b):
    batch, seq, d = x.shape
    assert w_in.shape[0] == 1, "one layer"
    n_exp = w_router.shape[-1]
    cap = EC_CAPACITY * seq // n_exp
    x2 = x.reshape(batch * seq, d)

    zn, zd1, zd2 = _in_proj(x2, w_in[0].astype(BF16), batch, seq)
    zt = _zn_tile(zn.shape[1] // BRANCH_WIDTH - 6)
    zn4 = zn.reshape(batch, 1, seq, zn.shape[1])
    o, lse = zip(_band_attention(zn4, (zt["qa"], zt["ka"], zt["va"]), batch, seq, 0),
                 _band_attention(zd1, (0, 1, 2), batch, seq, 1),
                 _band_attention(zd2, (0, 1, 2), batch, seq, 2))
    ob = _diff_attention(zn, zt, batch, seq, lambda_q1, lambda_k1, lambda_q2, lambda_k2, diff_norm_w)

    wr = jnp.pad(w_router[0], ((0, 0), (0, AFF_PAD - n_exp))).astype(BF16)
    x1, yext = _post(o, lse, ob, zn, x2, w_branch[0].astype(BF16), w_out[0].astype(BF16),
                     ln1_g, ln1_b, wr, n_exp)
    idx = _topk(yext, batch, seq, d, n_exp, cap)
    idx_flat = idx.transpose(1, 0, 2).reshape(-1)
    yext = _expert_ffn(idx_flat, x1, yext, w_gate[0], w_up[0], w_down[0], cap)
    return _ln2(yext, d, ln2_g, ln2_b).reshape(batch, seq, d)
```

```python
import functools
import math

import jax
import jax.numpy as jnp
from jax import lax
from jax.experimental import pallas as pl
from jax.experimental.pallas import tpu as pltpu

F32 = jnp.float32
BF16 = jnp.bfloat16

HEAD_DIM = 128
DIL_GROUPS = ((128, 1), (512, 4), (2048, 16))
HEADS_PER_GROUP = 4
N_GROUPS = len(DIL_GROUPS)
N_DIFF_HEADS = 4
BRANCH_WIDTH = HEADS_PER_GROUP * HEAD_DIM
ROPE_THETA = 500000.0
ROPE_FRACTION = 4
EC_CAPACITY = 2
LN_EPS = 1e-5
DIFF_NORM_EPS = 1e-5
NEG_INF = -1e30
LAMBDA_INIT = 0.8 - 0.6 * math.exp(-0.3 * 0)
ALPHA = 2.0 ** 0.25
DIFF_Q_SCALE = (HEAD_DIM // 2) ** -0.5 * math.log2(math.e)

N_DIL_TILES = 3 * N_GROUPS
QB_TILE, KB_TILE, VB_TILE = N_DIL_TILES, N_DIL_TILES + 1, N_DIL_TILES + 2
GATE_TILE = N_DIL_TILES + 3
LANES = 128
AFF_PAD = LANES
BAND_TQ = 128
VMEM_LIMIT = 56 * 1024 * 1024


def _cparams(sem):
    return pltpu.CompilerParams(dimension_semantics=sem, vmem_limit_bytes=VMEM_LIMIT)


def _rope_tables(seq, dh):
    rot = dh // ROPE_FRACTION
    half = rot // 2
    inv_freq = ROPE_THETA ** (-2.0 * jnp.arange(half, dtype=F32) / rot)
    ang = jnp.arange(seq).astype(F32)[:, None] * inv_freq[None, :]
    cos, sin = jnp.cos(ang), jnp.sin(ang)
    one = jnp.ones((seq, dh - rot), F32)
    zero = lambda n: jnp.zeros((seq, n), F32)
    c = jnp.concatenate([cos, cos, one], axis=-1)
    s_neg = jnp.concatenate([-sin, zero(dh - half)], axis=-1)
    s_pos = jnp.concatenate([zero(half), sin, zero(dh - rot)], axis=-1)
    tab = jnp.stack([c, s_neg, s_pos])
    return jnp.tile(tab, (1, 1, LANES // dh)), half


def _slabs(acc):
    return [acc[:, h * LANES:(h + 1) * LANES] for h in range(acc.shape[1] // LANES)]


def _in_proj_kernel(x_ref, w_ref, ra_ref, rb_ref, zn_ref, zd1_ref, zd2_ref, xb_ref, slab_ref, *, half_a, half_b):
    j = pl.program_id(1)
    tm = xb_ref.shape[0]
    sub = tm // 2

    @pl.when(j == 0)
    def _():
        xb_ref[...] = x_ref[...].astype(BF16)

    is_dil = j < N_DIL_TILES
    grp = j % N_GROUPS
    dil_rope = j < 2 * N_GROUPS
    is_b = jnp.logical_or(j == QB_TILE, j == KB_TILE)

    def project(rs):
        return _slabs(jnp.dot(xb_ref[rs, :], w_ref[...].astype(BF16), preferred_element_type=F32))

    def natural(rope):
        for k in range(2):
            rs = pl.ds(k * sub, sub)
            zn_ref[rs, :] = jnp.concatenate([rope(a, rs) for a in project(rs)], axis=1).astype(BF16)

    def phase_major(rope, out_ref, r):
        rows = sub // r
        for k in range(2):
            rs = pl.ds(k * sub, sub)
            slabs = project(rs)
            for h, a in enumerate(slabs):
                slab_ref[h, rs, :] = rope(a, rs)
            for p in range(r):
                out_ref[0, p, pl.ds(k * rows, rows), :] = jnp.concatenate(
                    [slab_ref[h, pl.ds(k * sub + p, rows, stride=r), :] for h in range(len(slabs))],
                    axis=1).astype(BF16)

    def rope_with(tab_ref, half, scale=None):
        def rope(a, rs):
            out = (a * tab_ref[0, rs, :] + pltpu.roll(a, LANES - half, 1) * tab_ref[1, rs, :]
                   + pltpu.roll(a, half, 1) * tab_ref[2, rs, :])
            return out if scale is None else out * scale
        return rope

    no_rope = lambda a, rs: a

    @pl.when(is_dil & dil_rope & (grp == 0))
    def _():
        natural(rope_with(ra_ref, half_a))

    for rope, cond in ((rope_with(ra_ref, half_a), dil_rope), (no_rope, jnp.logical_not(dil_rope))):
        @pl.when(is_dil & cond & (grp == 1))
        def _(rope=rope):
            phase_major(rope, zd1_ref, DIL_GROUPS[1][1])

        @pl.when(is_dil & cond & (grp == 2))
        def _(rope=rope):
            phase_major(rope, zd2_ref, DIL_GROUPS[2][1])

    @pl.when(is_b)
    def _():
        natural(rope_with(rb_ref, half_b, jnp.where(j == QB_TILE, DIFF_Q_SCALE, 1.0).astype(F32)))

    @pl.when(jnp.logical_not(is_b) & jnp.logical_not(is_dil & (dil_rope | (grp != 0))))
    def _():
        zn_ref[...] = jnp.dot(xb_ref[...], w_ref[...].astype(BF16), preferred_element_type=F32).astype(BF16)


def _zn_tile(n_gate_tiles):
    return dict(qa=n_gate_tiles, ka=n_gate_tiles + 1, va=n_gate_tiles + 2,
                qb=n_gate_tiles + 3, kb=n_gate_tiles + 4, vb=n_gate_tiles + 5)


def _in_proj(x2, w, batch, seq, tm=1024):
    t, d = x2.shape
    n_tiles = w.shape[1] // BRANCH_WIDTH
    n_gate_tiles = n_tiles - GATE_TILE
    tn = BRANCH_WIDTH
    tm = min(tm, seq)
    spt = seq // tm
    ra, half_a = _rope_tables(seq, HEAD_DIM)
    rb, half_b = _rope_tables(seq, HEAD_DIM // 2)
    r1, r2 = DIL_GROUPS[1][1], DIL_GROUPS[2][1]
    zt = _zn_tile(n_gate_tiles)

    def zn_col(j):
        return jnp.where(j >= GATE_TILE, j - GATE_TILE,
                         jnp.where(j >= N_DIL_TILES, j - N_DIL_TILES + zt["qb"], zt["qa"] + j // N_GROUPS))

    def zd_col(g):
        return lambda j: jnp.clip((j - g) // N_GROUPS, 0, 2)

    return pl.pallas_call(
        functools.partial(_in_proj_kernel, half_a=half_a, half_b=half_b),
        out_shape=(
            jax.ShapeDtypeStruct((t, (n_gate_tiles + 6) * tn), BF16),
            jax.ShapeDtypeStruct((batch, r1, seq // r1, 3 * tn), BF16),
            jax.ShapeDtypeStruct((batch, r2, seq // r2, 3 * tn), BF16),
        ),
        grid=(t // tm, n_tiles),
        in_specs=[
            pl.BlockSpec((tm, d), lambda i, j: (i, 0)),
            pl.BlockSpec((d, tn), lambda i, j: (0, j)),
            pl.BlockSpec((3, tm, LANES), lambda i, j: (0, i % spt, 0)),
            pl.BlockSpec((3, tm, LANES), lambda i, j: (0, i % spt, 0)),
        ],
        out_specs=(
            pl.BlockSpec((tm, tn), lambda i, j: (i, zn_col(j))),
            pl.BlockSpec((1, r1, tm // r1, tn), lambda i, j: (i // spt, 0, i % spt, zd_col(1)(j))),
            pl.BlockSpec((1, r2, tm // r2, tn), lambda i, j: (i // spt, 0, i % spt, zd_col(2)(j))),
        ),
        scratch_shapes=[pltpu.VMEM((tm, d), BF16), pltpu.VMEM((tn // LANES, tm, LANES), F32)],
        compiler_params=_cparams(("arbitrary", "arbitrary")),
        name="in_proj",
    )(x2, w, ra, rb)


def _band_kernel(q_ref, k_ref, v_ref, o_ref, lse_ref, *, seq_l, tl, r, half, scale):
    li = pl.program_id(1)
    win = BAND_TQ + 2 * half
    row = lax.broadcasted_iota(jnp.int32, (BAND_TQ, win), 0)
    col = lax.broadcasted_iota(jnp.int32, (BAND_TQ, win), 1)
    tiles = tl // BAND_TQ

    def body(it, carry):
        p = it // tiles
        t = it % tiles
        r0 = pl.multiple_of(t * BAND_TQ, BAND_TQ)
        q0 = li * tl + r0
        ws = pl.multiple_of(jnp.clip(q0 - half, 0, seq_l - win), half)
        valid = jnp.abs((q0 + row) - (ws + col)) <= half
        for h in range(HEADS_PER_GROUP):
            hs = slice(h * HEAD_DIM, (h + 1) * HEAD_DIM)
            q = q_ref[0, p, pl.ds(r0, BAND_TQ), hs]
            k = k_ref[0, p, pl.ds(ws, win), hs]
            v = v_ref[0, p, pl.ds(ws, win), hs]
            s = lax.dot_general(q, k, (((1,), (1,)), ((), ())), preferred_element_type=F32) * scale
            s = jnp.where(valid, s, NEG_INF)
            m = jnp.max(s, axis=-1, keepdims=True)
            pr = jnp.exp(s - m)
            den = jnp.sum(pr, axis=-1, keepdims=True)
            acc = jnp.dot(pr.astype(BF16), v, preferred_element_type=F32)
            dst = pl.ds(r0 * r + p, BAND_TQ, stride=r) if r > 1 else pl.ds(r0, BAND_TQ)
            o_ref[h, dst, :] = acc / den
            lse_ref[h, dst, :] = jnp.broadcast_to(m + jnp.log(den), (BAND_TQ, HEAD_DIM))
        return carry

    lax.fori_loop(0, r * tiles, body, 0, unroll=4)


def _band_attention(zsrc, tiles_qkv, batch, seq, g):
    window, r = DIL_GROUPS[g]
    half = window // (2 * r)
    seq_l = seq // r
    tl = max(BAND_TQ, 1024 // r)
    tq_, tk_, tv_ = tiles_qkv
    t = batch * seq
    nl = seq_l // tl
    out_sd = jax.ShapeDtypeStruct((HEADS_PER_GROUP, t, HEAD_DIM), F32)
    o_spec = pl.BlockSpec((HEADS_PER_GROUP, tl * r, HEAD_DIM), lambda b, li: (0, b * nl + li, 0))
    return pl.pallas_call(
        functools.partial(_band_kernel, seq_l=seq_l, tl=tl, r=r, half=half, scale=HEAD_DIM ** -0.5),
        out_shape=(out_sd, out_sd),
        grid=(batch, nl),
        in_specs=[
            pl.BlockSpec((1, r, tl, BRANCH_WIDTH), lambda b, li: (b, 0, li, tq_)),
            pl.BlockSpec((1, r, seq_l, BRANCH_WIDTH), lambda b, li: (b, 0, 0, tk_)),
            pl.BlockSpec((1, r, seq_l, BRANCH_WIDTH), lambda b, li: (b, 0, 0, tv_)),
        ],
        out_specs=(o_spec, o_spec),
        compiler_params=_cparams(("arbitrary", "arbitrary")),
        name=f"band_attn_g{g}",
    )(zsrc, zsrc, zsrc)


def _diff_kernel(lq1_ref, lk1_ref, lq2_ref, lk2_ref, nw_ref, q_ref, k_ref, v_ref, o_ref, s_ref, *, seq, tk):
    lam = (jnp.exp(jnp.sum(lq1_ref[...] * lk1_ref[...])) - jnp.exp(jnp.sum(lq2_ref[...] * lk2_ref[...]))
           + LAMBDA_INIT)
    dc = HEAD_DIM // 2
    q = q_ref[0]
    tq = q.shape[0]
    lane = lax.broadcasted_iota(jnp.int32, (tq, HEAD_DIM), 1)
    qc = [jnp.where(lane < dc, q, jnp.zeros_like(q)), jnp.where(lane >= dc, q, jnp.zeros_like(q))]
    slabs = tk // LANES

    def scores(j, mx):
        kj = k_ref[0, pl.ds(pl.multiple_of(j * tk, tk), tk), :]
        out = []
        for c in range(2):
            s = lax.dot_general(qc[c], kj, (((1,), (1,)), ((), ())), preferred_element_type=F32)
            s_ref[c, j] = s
            m = mx[c]
            for t in range(slabs):
                m = jnp.maximum(m, s[:, t * LANES:(t + 1) * LANES])
            out.append(m)
        return tuple(out)

    neg = jnp.full((tq, LANES), -jnp.inf, F32)
    mx = lax.fori_loop(0, seq // tk, scores, (neg, neg), unroll=2)
    row_max = [jnp.broadcast_to(jnp.max(m, axis=-1, keepdims=True), (tq, LANES)) for m in mx]

    ones = jnp.ones((tk, LANES), BF16)

    def weigh(j, accs):
        vj = jnp.concatenate([v_ref[0, pl.ds(pl.multiple_of(j * tk, tk), tk), :], ones], axis=1)
        out = []
        for c in range(2):
            p = [jnp.exp2(s_ref[c, j, :, t * LANES:(t + 1) * LANES] - row_max[c]).astype(BF16)
                 for t in range(slabs)]
            out.append(accs[c] + jnp.dot(jnp.concatenate(p, axis=1), vj, preferred_element_type=F32))
        return tuple(out)

    zero = jnp.zeros((tq, 2 * LANES), F32)
    acc0, acc1 = lax.fori_loop(0, seq // tk, weigh, (zero, zero), unroll=2)
    o = (acc0[:, :HEAD_DIM] / acc0[:, HEAD_DIM:]) - lam * (acc1[:, :HEAD_DIM] / acc1[:, HEAD_DIM:])
    o = o * lax.rsqrt(jnp.mean(o * o, axis=-1, keepdims=True) + DIFF_NORM_EPS)
    o_ref[0] = o * nw_ref[...] * (1.0 - LAMBDA_INIT)


def _diff_attention(zn, zt, batch, seq, lq1, lk1, lq2, lk2, norm_w, tq=1024, tk=2048):
    zv = zn.reshape(batch, seq, zn.shape[1])
    hpt = BRANCH_WIDTH // HEAD_DIM
    vec = lambda n: pl.BlockSpec((1, n), lambda b, h, qi: (0, 0))
    return pl.pallas_call(
        functools.partial(_diff_kernel, seq=seq, tk=tk),
        scratch_shapes=[pltpu.VMEM((2, seq // tk, tq, tk), F32)],
        out_shape=jax.ShapeDtypeStruct((batch, seq, BRANCH_WIDTH), F32),
        grid=(batch, N_DIFF_HEADS, seq // tq),
        in_specs=[
            vec(HEAD_DIM // 2), vec(HEAD_DIM // 2), vec(HEAD_DIM // 2), vec(HEAD_DIM // 2), vec(HEAD_DIM),
            pl.BlockSpec((1, tq, HEAD_DIM), lambda b, h, qi: (b, qi, zt["qb"] * hpt + h)),
            pl.BlockSpec((1, seq, HEAD_DIM), lambda b, h, qi: (b, 0, zt["kb"] * hpt + h)),
            pl.BlockSpec((1, seq, HEAD_DIM), lambda b, h, qi: (b, 0, zt["vb"] * hpt + h)),
        ],
        out_specs=pl.BlockSpec((1, tq, HEAD_DIM), lambda b, h, qi: (b, qi, h)),
        compiler_params=_cparams(("arbitrary", "arbitrary", "arbitrary")),
        name="diff_attn",
    )(lq1, lk1, lq2, lk2, norm_w, zv, zv, zv).reshape(batch * seq, BRANCH_WIDTH)


def _layer_norm(h, g, b):
    mu = jnp.mean(h, axis=-1, keepdims=True)
    hc = h - mu
    var = jnp.mean(hc * hc, axis=-1, keepdims=True)
    return hc * lax.rsqrt(var + LN_EPS) * g + b


def _post_kernel(o0_ref, o1_ref, o2_ref, l0_ref, l1_ref, l2_ref, ob_ref, g0_ref, g1_ref, x_ref,
                 wb_ref, wo_ref, lng_ref, lnb_ref, wr_ref, x1_ref, yext_ref, *, n_exp):
    d = x_ref.shape[1]
    heads = lambda ref: jnp.concatenate([ref[h] for h in range(HEADS_PER_GROUP)], axis=1)
    l0, l1, l2 = heads(l0_ref), heads(l1_ref), heads(l2_ref)
    lm = jnp.maximum(jnp.maximum(l0, l1), l2)
    e0, e1, e2 = jnp.exp(l0 - lm), jnp.exp(l1 - lm), jnp.exp(l2 - lm)
    den = e0 + e1 + e2
    oa = heads(o0_ref) * (e0 / den) + heads(o1_ref) * (e1 / den) + heads(o2_ref) * (e2 / den)
    bd0 = jnp.dot(oa.astype(BF16), wb_ref[0], preferred_element_type=F32)
    bd1 = jnp.dot(ob_ref[...].astype(BF16), wb_ref[1], preferred_element_type=F32)
    merged = jax.nn.sigmoid(g0_ref[...].astype(F32)) * bd0 + jax.nn.sigmoid(g1_ref[...].astype(F32)) * bd1
    mix = jnp.dot(merged.astype(BF16), wo_ref[...], preferred_element_type=F32)
    x1 = _layer_norm(ALPHA * x_ref[...] + mix, lng_ref[...], lnb_ref[...])
    logits = jnp.dot(x1.astype(BF16), wr_ref[...], preferred_element_type=F32)
    lane = lax.broadcasted_iota(jnp.int32, logits.shape, 1)
    logits = jnp.where(lane < n_exp, logits, -jnp.inf)
    ex = jnp.exp(logits - jnp.max(logits, axis=-1, keepdims=True))
    aff = ex / jnp.sum(ex, axis=-1, keepdims=True)
    x1_ref[...] = x1
    yext_ref[:, :d] = ALPHA * x1
    yext_ref[:, d:] = aff


def _post(o, lse, ob, zn, x2, wb, wo, lng, lnb, wr, n_exp, tm=256):
    t, d = x2.shape
    row = lambda w: pl.BlockSpec((tm, w), lambda i: (i, 0))
    hm = pl.BlockSpec((HEADS_PER_GROUP, tm, HEAD_DIM), lambda i: (0, i, 0))
    full = lambda a: pl.BlockSpec(a.shape, lambda i: (0,) * a.ndim)
    return pl.pallas_call(
        functools.partial(_post_kernel, n_exp=n_exp),
        out_shape=(jax.ShapeDtypeStruct((t, d), F32), jax.ShapeDtypeStruct((t, d + AFF_PAD), F32)),
        grid=(t // tm,),
        in_specs=[hm] * 6 + [
            row(BRANCH_WIDTH),
            pl.BlockSpec((tm, d), lambda i: (i, 0)),
            pl.BlockSpec((tm, d), lambda i: (i, 1)),
            row(d), full(wb), full(wo), full(lng), full(lnb), full(wr),
        ],
        out_specs=(row(d), row(d + AFF_PAD)),
        compiler_params=_cparams(("arbitrary",)),
        name="post_attn",
    )(o[0], o[1], o[2], lse[0], lse[1], lse[2], ob, zn, zn, x2, wb, wo, lng, lnb, wr)


def _topk_kernel(aff_ref, idx_ref, loc_ref, *, seq, cap, n_exp):
    b = pl.program_id(0)
    nb = seq // LANES
    blocks = [aff_ref[k * LANES:(k + 1) * LANES, :].T[:n_exp, :] for k in range(nb)]
    lane = lax.broadcasted_iota(jnp.int32, (n_exp, LANES), 1)
    i0 = lax.broadcasted_iota(jnp.int32, (LANES, LANES), 0)
    i1 = lax.broadcasted_iota(jnp.int32, (LANES, LANES), 1)
    incl = (i0 <= i1).astype(BF16)
    excl = (i0 < i1).astype(BF16)

    def count(pred):
        acc = jnp.zeros((n_exp, LANES), F32)
        for blk in blocks:
            acc = acc + pred(blk).astype(F32)
        return jnp.sum(acc, axis=1, keepdims=True)

    def search(i, t):
        cand = t | jnp.left_shift(jnp.int32(1), 30 - i)
        cf = pltpu.bitcast(cand, F32)
        return jnp.where(count(lambda blk: blk >= cf) >= cap, cand, t)

    thr = pltpu.bitcast(lax.fori_loop(0, 31, search, jnp.zeros((n_exp, LANES), jnp.int32)), F32)
    need = cap - count(lambda blk: blk > thr)

    def running(masks):
        local = [jnp.dot(m.astype(BF16), incl, preferred_element_type=F32) for m in masks]
        tot = jnp.zeros((n_exp, LANES), F32)
        for k, part in enumerate(local):
            tot = jnp.where(lane == k, part[:, LANES - 1:LANES], tot)
        return local, tot, jnp.dot(tot.astype(BF16), excl, preferred_element_type=F32)

    eq_local, _, eq_off = running([blk == thr for blk in blocks])
    sel = [(blk > thr) | ((blk == thr) & (eq_local[k] + eq_off[:, k:k + 1] <= need))
           for k, blk in enumerate(blocks)]
    sel_local, sel_tot, sel_off = running(sel)
    for k, part in enumerate(sel_local):
        loc_ref[k * n_exp:(k + 1) * n_exp, :] = part
    cum_end = sel_off + sel_tot

    slot = lax.broadcasted_iota(jnp.int32, (cap, 1), 0).astype(F32)
    lane_c = lax.broadcasted_iota(jnp.int32, (cap, LANES), 1)
    lane_f = lane_c.astype(F32)
    pad = jnp.zeros((LANES - nb, LANES), F32)
    res = jnp.zeros((cap, LANES), F32)
    for e in range(n_exp):
        blk_of = jnp.sum((cum_end[e:e + 1, :] <= slot).astype(F32), axis=1, keepdims=True)
        onehot = lane_f == blk_of
        before = jnp.sum(jnp.where(onehot, sel_off[e:e + 1, :], 0.0), axis=1, keepdims=True)
        table = jnp.concatenate([loc_ref[pl.ds(e, nb, stride=n_exp), :], pad], axis=0).astype(BF16)
        local = jnp.dot(onehot.astype(BF16), table, preferred_element_type=F32)
        pos = jnp.sum((local <= slot - before).astype(F32), axis=1, keepdims=True)
        res = jnp.where(lane_c == e, blk_of * LANES + pos, res)
    idx_ref[0] = res.T[:n_exp, :].astype(jnp.int32) + b * seq


def _topk(yext, batch, seq, d, n_exp, cap):
    assert n_exp % 8 == 0 and seq // LANES <= LANES and cap % LANES == 0
    return pl.pallas_call(
        functools.partial(_topk_kernel, seq=seq, cap=cap, n_exp=n_exp),
        out_shape=jax.ShapeDtypeStruct((batch, n_exp, cap), jnp.int32),
        grid=(batch,),
        in_specs=[pl.BlockSpec((seq, AFF_PAD), lambda b: (b, d // AFF_PAD))],
        out_specs=pl.BlockSpec((1, n_exp, cap), lambda b: (b, 0, 0)),
        scratch_shapes=[pltpu.VMEM((seq // LANES * n_exp, LANES), F32)],
        compiler_params=_cparams(("arbitrary",)),
        name="expert_topk",
    )(yext)


def _ffn_kernel(idx_ref, x1_hbm, yin_hbm, wg_ref, wu_ref, wd_ref, yext_hbm, xg_ref, xb_ref, yg_ref, acc_ref, sems,
                *, rows, d, n_split, nf):
    del yin_hbm
    e, hf, f = pl.program_id(0), pl.program_id(1), pl.program_id(2)
    n_groups = pl.num_programs(0) * n_split
    grp = e * n_split + hf
    base = grp * rows
    nxt = jnp.minimum(grp + 1, n_groups - 1) * rows
    prev = jnp.where(grp > 0, base - rows, rows)
    per = rows // nf
    yg_cur, yg_prev = yg_ref.at[grp % 2], yg_ref.at[(grp + 1) % 2]

    def x_copy(r, tok):
        return pltpu.make_async_copy(x1_hbm.at[pl.ds(tok, 1)], xg_ref.at[pl.ds(r, 1)], sems.at[0])

    def y_in_copy(buf, r, tok):
        return pltpu.make_async_copy(yext_hbm.at[pl.ds(tok, 1)], buf.at[pl.ds(r, 1)], sems.at[1])

    def y_out_copy(buf, r, tok):
        return pltpu.make_async_copy(buf.at[pl.ds(r, 1)], yext_hbm.at[pl.ds(tok, 1)], sems.at[2])

    def for_rows(fn, start):
        def body(r, c):
            fn(r, idx_ref[start + r])
            return c
        lax.fori_loop(0, rows, body, 0, unroll=8)

    @pl.when((f == 0) & (grp == 0))
    def _():
        for_rows(lambda r, tok: x_copy(r, tok).start(), base)
        for_rows(lambda r, tok: y_in_copy(yg_prev, r, tok).start(), prev)
        for_rows(lambda r, tok: y_in_copy(yg_prev, r, tok).wait(), prev)

    @pl.when(f == 0)
    def _():
        for_rows(lambda r, tok: x_copy(r, tok).wait(), base)
        xb_ref[...] = xg_ref[...].astype(BF16)
        acc_ref[...] = jnp.zeros_like(acc_ref)

    for r in range(per):
        row = f * per + r
        x_copy(row, idx_ref[nxt + row]).start()
        y_in_copy(yg_cur, row, idx_ref[base + row]).start()
        y_out_copy(yg_prev, row, idx_ref[prev + row]).start()

    wg, wu, wd = wg_ref[0].astype(BF16), wu_ref[0].astype(BF16), wd_ref[0].astype(BF16)
    sub = rows // 2
    for rs in (pl.ds(0, sub), pl.ds(sub, sub)):
        x = xb_ref[rs, :]
        g = jnp.dot(x, wg, preferred_element_type=F32)
        u = jnp.dot(x, wu, preferred_element_type=F32)
        hid = (g * jax.nn.sigmoid(g)) * u
        acc_ref[rs, :] += jnp.dot(hid.astype(BF16), wd, preferred_element_type=F32)

    @pl.when(f == nf - 1)
    def _():
        for_rows(lambda r, tok: y_in_copy(yg_cur, r, tok).wait(), base)
        for_rows(lambda r, tok: y_out_copy(yg_prev, r, tok).wait(), prev)
        aff = yg_cur[:, d:]
        lane = lax.broadcasted_iota(jnp.int32, aff.shape, 1)
        gate = jnp.sum(jnp.where(lane == e, aff, 0.0), axis=-1, keepdims=True)
        yg_cur[:, :d] = yg_cur[:, :d] + acc_ref[...] * gate

    @pl.when((f == nf - 1) & (grp == n_groups - 1))
    def _():
        for_rows(lambda r, tok: x_copy(r, tok).wait(), base)
        for_rows(lambda r, tok: y_out_copy(yg_cur, r, tok).start(), base)
        for_rows(lambda r, tok: y_out_copy(yg_cur, r, tok).wait(), base)


def _expert_ffn(idx_flat, x1, yext, w_gate, w_up, w_down, cap, tf=256, n_split=2):
    t, d = x1.shape
    n_exp, _, ff = w_gate.shape
    tf = min(tf, ff)
    nf = ff // tf
    rows = idx_flat.shape[0] // (n_exp * n_split)
    assert rows % nf == 0 and n_split == 2 and rows % cap == 0
    grid_spec = pltpu.PrefetchScalarGridSpec(
        num_scalar_prefetch=1,
        grid=(n_exp, n_split, nf),
        in_specs=[
            pl.BlockSpec(memory_space=pl.ANY),
            pl.BlockSpec(memory_space=pl.ANY),
            pl.BlockSpec((1, d, tf), lambda e, h, f, idx: (e, 0, f)),
            pl.BlockSpec((1, d, tf), lambda e, h, f, idx: (e, 0, f)),
            pl.BlockSpec((1, tf, d), lambda e, h, f, idx: (e, f, 0)),
        ],
        out_specs=pl.BlockSpec(memory_space=pl.ANY),
        scratch_shapes=[
            pltpu.VMEM((rows, d), F32),
            pltpu.VMEM((rows, d), BF16),
            pltpu.VMEM((2, rows, d + AFF_PAD), F32),
            pltpu.VMEM((rows, d), F32),
            pltpu.SemaphoreType.DMA((3,)),
        ],
    )
    return pl.pallas_call(
        functools.partial(_ffn_kernel, rows=rows, d=d, n_split=n_split, nf=nf),
        out_shape=jax.ShapeDtypeStruct(yext.shape, yext.dtype),
        grid_spec=grid_spec,
        input_output_aliases={2: 0},
        compiler_params=_cparams(("arbitrary", "arbitrary", "arbitrary")),
        name="expert_ffn",
    )(idx_flat, x1, yext, w_gate, w_up, w_down)


def _ln2_kernel(y_ref, g_ref, b_ref, o_ref):
    o_ref[...] = _layer_norm(y_ref[...], g_ref[...], b_ref[...])


def _ln2(yext, d, g, b, tm=512):
    t = yext.shape[0]
    return pl.pallas_call(
        _ln2_kernel,
        out_shape=jax.ShapeDtypeStruct((t, d), F32),
        grid=(t // tm,),
        in_specs=[pl.BlockSpec((tm, d), lambda i: (i, 0)),
                  pl.BlockSpec((1, d), lambda i: (0, 0)), pl.BlockSpec((1, d), lambda i: (0, 0))],
        out_specs=pl.BlockSpec((tm, d), lambda i: (i, 0)),
        compiler_params=_cparams(("arbitrary",)),
        name="ln2",
    )(yext, g, b)


def kernel(x, w_in, lambda_q1, lambda_k1, lambda_q2, lambda_k2, diff_norm_w, w_branch, w_out, ln1_g, ln1_b,
           w_router, w_gate, w_up, w_down, ln2_g, ln2_b):
    batch, seq, d = x.shape
    assert w_in.shape[0] == 1, "one layer"
    n_exp = w_router.shape[-1]
    cap = EC_CAPACITY * seq // n_exp
    x2 = x.reshape(batch * seq, d)

    zn, zd1, zd2 = _in_proj(x2, w_in[0], batch, seq)
    zt = _zn_tile(zn.shape[1] // BRANCH_WIDTH - 6)
    zn4 = zn.reshape(batch, 1, seq, zn.shape[1])
    o, lse = zip(_band_attention(zn4, (zt["qa"], zt["ka"], zt["va"]), batch, seq, 0),
                 _band_attention(zd1, (0, 1, 2), batch, seq, 1),
                 _band_attention(zd2, (0, 1, 2), batch, seq, 2))
    ob = _diff_attention(zn, zt, batch, seq, lambda_q1, lambda_k1, lambda_q2, lambda_k2, diff_norm_w)

    wr = jnp.pad(w_router[0], ((0, 0), (0, AFF_PAD - n_exp))).astype(BF16)
    x1, yext = _post(o, lse, ob, zn, x2, w_branch[0].astype(BF16), w_out[0].astype(BF16),
                     ln1_g, ln1_b, wr, n_exp)
    idx = _topk(yext, batch, seq, d, n_exp, cap)
    idx_flat = idx.transpose(1, 0, 2).reshape(-1)
    yext = _expert_ffn(idx_flat, x1, yext, w_gate[0], w_up[0], w_down[0], cap)
    return _ln2(yext, d, ln2_g, ln2_b).reshape(batch, seq, d)
```

```python
import functools
import math

import jax
import jax.numpy as jnp
from jax import lax
from jax.experimental import pallas as pl
from jax.experimental.pallas import tpu as pltpu

F32 = jnp.float32
BF16 = jnp.bfloat16

HEAD_DIM = 128
DIL_GROUPS = ((128, 1), (512, 4), (2048, 16))
HEADS_PER_GROUP = 4
N_GROUPS = len(DIL_GROUPS)
N_DIFF_HEADS = 4
BRANCH_WIDTH = HEADS_PER_GROUP * HEAD_DIM
ROPE_THETA = 500000.0
ROPE_FRACTION = 4
EC_CAPACITY = 2
LN_EPS = 1e-5
DIFF_NORM_EPS = 1e-5
NEG_INF = -1e30
LAMBDA_INIT = 0.8 - 0.6 * math.exp(-0.3 * 0)
ALPHA = 2.0 ** 0.25
DIFF_Q_SCALE = (HEAD_DIM // 2) ** -0.5 * math.log2(math.e)
BAND_Q_SCALE = HEAD_DIM ** -0.5 * math.log2(math.e)
LN2 = math.log(2.0)

N_DIL_TILES = 3 * N_GROUPS
QB_TILE, KB_TILE, VB_TILE = N_DIL_TILES, N_DIL_TILES + 1, N_DIL_TILES + 2
GATE_TILE = N_DIL_TILES + 3
LANES = 128
AFF_PAD = LANES
BAND_TQ = 128
VMEM_LIMIT = 56 * 1024 * 1024


def _cparams(sem):
    return pltpu.CompilerParams(dimension_semantics=sem, vmem_limit_bytes=VMEM_LIMIT)


def _rope_tables(seq, dh):
    rot = dh // ROPE_FRACTION
    half = rot // 2
    inv_freq = ROPE_THETA ** (-2.0 * jnp.arange(half, dtype=F32) / rot)
    ang = jnp.arange(seq).astype(F32)[:, None] * inv_freq[None, :]
    cos, sin = jnp.cos(ang), jnp.sin(ang)
    one = jnp.ones((seq, dh - rot), F32)
    zero = lambda n: jnp.zeros((seq, n), F32)
    c = jnp.concatenate([cos, cos, one], axis=-1)
    s_neg = jnp.concatenate([-sin, zero(dh - half)], axis=-1)
    s_pos = jnp.concatenate([zero(half), sin, zero(dh - rot)], axis=-1)
    tab = jnp.stack([c, s_neg, s_pos])
    return jnp.tile(tab, (1, 1, LANES // dh)), half


def _slabs(acc):
    return [acc[:, h * LANES:(h + 1) * LANES] for h in range(acc.shape[1] // LANES)]


def _in_proj_kernel(x_ref, w_ref, w2_ref, ra_ref, rb_ref, zn_ref, zg_ref, zd1_ref, zd2_ref, xb_ref, slab_ref,
                    *, half_a, half_b):
    j = pl.program_id(1)
    tm = xb_ref.shape[0]
    sub = tm // 2

    @pl.when(j == 0)
    def _():
        xb_ref[...] = x_ref[...].astype(BF16)

    is_gate = j >= GATE_TILE
    is_dil = j < N_DIL_TILES
    grp = j % N_GROUPS
    dil_rope = j < 2 * N_GROUPS
    is_b = jnp.logical_or(j == QB_TILE, j == KB_TILE)

    def project(rs):
        return _slabs(jnp.dot(xb_ref[rs, :], w_ref[...].astype(BF16), preferred_element_type=F32))

    @pl.when(is_gate)
    def _():
        w2 = jnp.concatenate([w_ref[...].astype(BF16), w2_ref[...].astype(BF16)], axis=1)
        zg_ref[...] = jnp.dot(xb_ref[...], w2, preferred_element_type=F32).astype(BF16)

    def natural(rope):
        for k in range(2):
            rs = pl.ds(k * sub, sub)
            zn_ref[rs, :] = jnp.concatenate([rope(a, rs) for a in project(rs)], axis=1).astype(BF16)

    def phase_major(rope, out_ref, r):
        rows = sub // r
        for k in range(2):
            rs = pl.ds(k * sub, sub)
            slabs = project(rs)
            for h, a in enumerate(slabs):
                slab_ref[h, rs, :] = rope(a, rs)
            for p in range(r):
                out_ref[0, p, pl.ds(k * rows, rows), :] = jnp.concatenate(
                    [slab_ref[h, pl.ds(k * sub + p, rows, stride=r), :] for h in range(len(slabs))],
                    axis=1).astype(BF16)

    def rope_with(tab_ref, half, scale=None):
        def rope(a, rs):
            out = (a * tab_ref[0, rs, :] + pltpu.roll(a, LANES - half, 1) * tab_ref[1, rs, :]
                   + pltpu.roll(a, half, 1) * tab_ref[2, rs, :])
            return out if scale is None else out * scale
        return rope

    no_rope = lambda a, rs: a

    rope_qk = rope_with(ra_ref, half_a, jnp.where(j < N_GROUPS, BAND_Q_SCALE, 1.0).astype(F32))

    @pl.when(is_dil & dil_rope & (grp == 0))
    def _():
        natural(rope_qk)

    for rope, cond in ((rope_qk, dil_rope), (no_rope, jnp.logical_not(dil_rope))):
        @pl.when(is_dil & cond & (grp == 1))
        def _(rope=rope):
            phase_major(rope, zd1_ref, DIL_GROUPS[1][1])

        @pl.when(is_dil & cond & (grp == 2))
        def _(rope=rope):
            phase_major(rope, zd2_ref, DIL_GROUPS[2][1])

    @pl.when(is_b)
    def _():
        natural(rope_with(rb_ref, half_b, jnp.where(j == QB_TILE, DIFF_Q_SCALE, 1.0).astype(F32)))

    @pl.when(jnp.logical_not(is_gate | is_b) & jnp.logical_not(is_dil & (dil_rope | (grp != 0))))
    def _():
        zn_ref[...] = jnp.dot(xb_ref[...], w_ref[...].astype(BF16), preferred_element_type=F32).astype(BF16)


ZN_TILE = dict(qa=0, ka=1, va=2, qb=3, kb=4, vb=5)


def _in_proj(x2, w, batch, seq, tm=1024):
    t, d = x2.shape
    n_tiles = w.shape[1] // BRANCH_WIDTH
    n_gate_tiles = n_tiles - GATE_TILE
    assert n_gate_tiles % 2 == 0
    tn = BRANCH_WIDTH
    tm = min(tm, seq)
    spt = seq // tm
    ra, half_a = _rope_tables(seq, HEAD_DIM)
    rb, half_b = _rope_tables(seq, HEAD_DIM // 2)
    r1, r2 = DIL_GROUPS[1][1], DIL_GROUPS[2][1]
    pair = lambda j: jnp.maximum(j - GATE_TILE, 0)

    def zn_col(j):
        jc = jnp.minimum(j, GATE_TILE - 1)
        return jnp.where(jc >= N_DIL_TILES, jc - N_DIL_TILES + ZN_TILE["qb"], ZN_TILE["qa"] + jc // N_GROUPS)

    def zd_col(g):
        return lambda j: jnp.clip((j - g) // N_GROUPS, 0, 2)

    return pl.pallas_call(
        functools.partial(_in_proj_kernel, half_a=half_a, half_b=half_b),
        out_shape=(
            jax.ShapeDtypeStruct((t, 6 * tn), BF16),
            jax.ShapeDtypeStruct((t, n_gate_tiles * tn), BF16),
            jax.ShapeDtypeStruct((batch, r1, seq // r1, 3 * tn), BF16),
            jax.ShapeDtypeStruct((batch, r2, seq // r2, 3 * tn), BF16),
        ),
        grid=(t // tm, GATE_TILE + n_gate_tiles // 2),
        in_specs=[
            pl.BlockSpec((tm, d), lambda i, j: (i, 0)),
            pl.BlockSpec((d, tn), lambda i, j: (0, jnp.where(j < GATE_TILE, j, GATE_TILE + 2 * pair(j)))),
            pl.BlockSpec((d, tn), lambda i, j: (0, GATE_TILE + 1 + 2 * pair(j))),
            pl.BlockSpec((3, tm, LANES), lambda i, j: (0, i % spt, 0), pipeline_mode=pl.Buffered(1)),
            pl.BlockSpec((3, tm, LANES), lambda i, j: (0, i % spt, 0), pipeline_mode=pl.Buffered(1)),
        ],
        out_specs=(
            pl.BlockSpec((tm, tn), lambda i, j: (i, zn_col(j))),
            pl.BlockSpec((tm, 2 * tn), lambda i, j: (i, pair(j))),
            pl.BlockSpec((1, r1, tm // r1, tn), lambda i, j: (i // spt, 0, i % spt, zd_col(1)(j))),
            pl.BlockSpec((1, r2, tm // r2, tn), lambda i, j: (i // spt, 0, i % spt, zd_col(2)(j))),
        ),
        scratch_shapes=[pltpu.VMEM((tm, d), BF16), pltpu.VMEM((tn // LANES, tm, LANES), F32)],
        compiler_params=_cparams(("arbitrary", "arbitrary")),
        name="in_proj",
    )(x2, w, w, ra, rb)


def _band_kernel(q_ref, k_ref, v_ref, o_ref, lse_ref, *, seq_l, tl, r, half):
    li = pl.program_id(1)
    win = BAND_TQ + 2 * half
    row = lax.broadcasted_iota(jnp.int32, (BAND_TQ, win), 0)
    col = lax.broadcasted_iota(jnp.int32, (BAND_TQ, win), 1)
    tiles = tl // BAND_TQ

    def body(it, carry):
        p = it // tiles
        t = it % tiles
        r0 = pl.multiple_of(t * BAND_TQ, BAND_TQ)
        q0 = li * tl + r0
        ws = pl.multiple_of(jnp.clip(q0 - half, 0, seq_l - win), half)
        valid = jnp.abs((q0 + row) - (ws + col)) <= half
        for h in range(HEADS_PER_GROUP):
            hs = slice(h * HEAD_DIM, (h + 1) * HEAD_DIM)
            q = q_ref[0, p, pl.ds(r0, BAND_TQ), hs]
            k = k_ref[0, p, pl.ds(ws, win), hs]
            v = v_ref[0, p, pl.ds(ws, win), hs]
            s = lax.dot_general(q, k, (((1,), (1,)), ((), ())), preferred_element_type=F32)
            s = jnp.where(valid, s, NEG_INF)
            m = jnp.max(s, axis=-1, keepdims=True)
            pr = jnp.exp2(s - m)
            den = jnp.sum(pr, axis=-1, keepdims=True)
            acc = jnp.dot(pr.astype(BF16), v, preferred_element_type=F32)
            dst = pl.ds(r0 * r + p, BAND_TQ, stride=r) if r > 1 else pl.ds(r0, BAND_TQ)
            o_ref[h, dst, :] = acc / den
            lse_ref[h, dst, :] = jnp.broadcast_to(m * LN2 + jnp.log(den), (BAND_TQ, HEAD_DIM))
        return carry

    lax.fori_loop(0, r * tiles, body, 0, unroll=4)


def _band_attention(zsrc, tiles_qkv, batch, seq, g):
    window, r = DIL_GROUPS[g]
    half = window // (2 * r)
    seq_l = seq // r
    tl = max(BAND_TQ, 1024 // r)
    tq_, tk_, tv_ = tiles_qkv
    t = batch * seq
    nl = seq_l // tl
    out_sd = jax.ShapeDtypeStruct((HEADS_PER_GROUP, t, HEAD_DIM), F32)
    o_spec = pl.BlockSpec((HEADS_PER_GROUP, tl * r, HEAD_DIM), lambda b, li: (0, b * nl + li, 0))
    return pl.pallas_call(
        functools.partial(_band_kernel, seq_l=seq_l, tl=tl, r=r, half=half),
        out_shape=(out_sd, out_sd),
        grid=(batch, nl),
        in_specs=[
            pl.BlockSpec((1, r, tl, BRANCH_WIDTH), lambda b, li: (b, 0, li, tq_)),
            pl.BlockSpec((1, r, seq_l, BRANCH_WIDTH), lambda b, li: (b, 0, 0, tk_)),
            pl.BlockSpec((1, r, seq_l, BRANCH_WIDTH), lambda b, li: (b, 0, 0, tv_)),
        ],
        out_specs=(o_spec, o_spec),
        compiler_params=_cparams(("arbitrary", "arbitrary")),
        name=f"band_attn_g{g}",
    )(zsrc, zsrc, zsrc)


def _diff_kernel(lq1_ref, lk1_ref, lq2_ref, lk2_ref, nw_ref, q_ref, k_ref, v_ref, o_ref, s_ref, *, seq, tk):
    lam = (jnp.exp(jnp.sum(lq1_ref[...] * lk1_ref[...])) - jnp.exp(jnp.sum(lq2_ref[...] * lk2_ref[...]))
           + LAMBDA_INIT)
    dc = HEAD_DIM // 2
    q = q_ref[0]
    tq = q.shape[0]
    lane = lax.broadcasted_iota(jnp.int32, (tq, HEAD_DIM), 1)
    qc = [jnp.where(lane < dc, q, jnp.zeros_like(q)), jnp.where(lane >= dc, q, jnp.zeros_like(q))]
    slabs = tk // LANES

    def scores(j, mx):
        kj = k_ref[0, pl.ds(pl.multiple_of(j * tk, tk), tk), :]
        out = []
        for c in range(2):
            s = lax.dot_general(qc[c], kj, (((1,), (1,)), ((), ())), preferred_element_type=F32)
            s_ref[c, j] = s
            m = mx[c]
            for t in range(slabs):
                m = jnp.maximum(m, s[:, t * LANES:(t + 1) * LANES])
            out.append(m)
        return tuple(out)

    neg = jnp.full((tq, LANES), -jnp.inf, F32)
    mx = lax.fori_loop(0, seq // tk, scores, (neg, neg), unroll=2)
    row_max = [jnp.broadcast_to(jnp.max(m, axis=-1, keepdims=True), (tq, LANES)) for m in mx]

    ones = jnp.ones((tk, LANES), BF16)

    def weigh(j, accs):
        vj = jnp.concatenate([v_ref[0, pl.ds(pl.multiple_of(j * tk, tk), tk), :], ones], axis=1)
        out = []
        for c in range(2):
            p = [jnp.exp2(s_ref[c, j, :, t * LANES:(t + 1) * LANES] - row_max[c]).astype(BF16)
                 for t in range(slabs)]
            out.append(accs[c] + jnp.dot(jnp.concatenate(p, axis=1), vj, preferred_element_type=F32))
        return tuple(out)

    zero = jnp.zeros((tq, 2 * LANES), F32)
    acc0, acc1 = lax.fori_loop(0, seq // tk, weigh, (zero, zero), unroll=2)
    o = (acc0[:, :HEAD_DIM] / acc0[:, HEAD_DIM:]) - lam * (acc1[:, :HEAD_DIM] / acc1[:, HEAD_DIM:])
    o = o * lax.rsqrt(jnp.mean(o * o, axis=-1, keepdims=True) + DIFF_NORM_EPS)
    o_ref[0] = o * nw_ref[...] * (1.0 - LAMBDA_INIT)


def _diff_attention(zn, batch, seq, lq1, lk1, lq2, lk2, norm_w, tq=1024, tk=2048):
    zt = ZN_TILE
    zv = zn.reshape(batch, seq, zn.shape[1])
    hpt = BRANCH_WIDTH // HEAD_DIM
    vec = lambda n: pl.BlockSpec((1, n), lambda b, h, qi: (0, 0))
    return pl.pallas_call(
        functools.partial(_diff_kernel, seq=seq, tk=tk),
        scratch_shapes=[pltpu.VMEM((2, seq // tk, tq, tk), F32)],
        out_shape=jax.ShapeDtypeStruct((batch, seq, BRANCH_WIDTH), F32),
        grid=(batch, N_DIFF_HEADS, seq // tq),
        in_specs=[
            vec(HEAD_DIM // 2), vec(HEAD_DIM // 2), vec(HEAD_DIM // 2), vec(HEAD_DIM // 2), vec(HEAD_DIM),
            pl.BlockSpec((1, tq, HEAD_DIM), lambda b, h, qi: (b, qi, zt["qb"] * hpt + h)),
            pl.BlockSpec((1, seq, HEAD_DIM), lambda b, h, qi: (b, 0, zt["kb"] * hpt + h)),
            pl.BlockSpec((1, seq, HEAD_DIM), lambda b, h, qi: (b, 0, zt["vb"] * hpt + h)),
        ],
        out_specs=pl.BlockSpec((1, tq, HEAD_DIM), lambda b, h, qi: (b, qi, h)),
        compiler_params=_cparams(("arbitrary", "arbitrary", "arbitrary")),
        name="diff_attn",
    )(lq1, lk1, lq2, lk2, norm_w, zv, zv, zv).reshape(batch * seq, BRANCH_WIDTH)


def _layer_norm(h, g, b):
    mu = jnp.mean(h, axis=-1, keepdims=True)
    hc = h - mu
    var = jnp.mean(hc * hc, axis=-1, keepdims=True)
    return hc * lax.rsqrt(var + LN_EPS) * g + b


def _post_kernel(o0_ref, o1_ref, o2_ref, l0_ref, l1_ref, l2_ref, ob_ref, g0_ref, g1_ref, x_ref,
                 wb_ref, wo_ref, lng_ref, lnb_ref, wr_ref, x1_ref, yext_ref, *, n_exp):
    d = x_ref.shape[1]
    heads = lambda ref: jnp.concatenate([ref[h] for h in range(HEADS_PER_GROUP)], axis=1)
    l0, l1, l2 = heads(l0_ref), heads(l1_ref), heads(l2_ref)
    lm = jnp.maximum(jnp.maximum(l0, l1), l2)
    e0, e1, e2 = jnp.exp(l0 - lm), jnp.exp(l1 - lm), jnp.exp(l2 - lm)
    den = e0 + e1 + e2
    oa = heads(o0_ref) * (e0 / den) + heads(o1_ref) * (e1 / den) + heads(o2_ref) * (e2 / den)
    bd0 = jnp.dot(oa.astype(BF16), wb_ref[0], preferred_element_type=F32)
    bd1 = jnp.dot(ob_ref[...].astype(BF16), wb_ref[1], preferred_element_type=F32)
    merged = jax.nn.sigmoid(g0_ref[...].astype(F32)) * bd0 + jax.nn.sigmoid(g1_ref[...].astype(F32)) * bd1
    mix = jnp.dot(merged.astype(BF16), wo_ref[...], preferred_element_type=F32)
    x1 = _layer_norm(ALPHA * x_ref[...] + mix, lng_ref[...], lnb_ref[...])
    logits = jnp.dot(x1.astype(BF16), wr_ref[...], preferred_element_type=F32)
    lane = lax.broadcasted_iota(jnp.int32, logits.shape, 1)
    logits = jnp.where(lane < n_exp, logits, -jnp.inf)
    ex = jnp.exp(logits - jnp.max(logits, axis=-1, keepdims=True))
    aff = ex / jnp.sum(ex, axis=-1, keepdims=True)
    x1_ref[...] = x1
    yext_ref[:, :d] = ALPHA * x1
    yext_ref[:, d:] = aff


def _post(o, lse, ob, zg, x2, wb, wo, lng, lnb, wr, n_exp, tm=256):
    t, d = x2.shape
    row = lambda w: pl.BlockSpec((tm, w), lambda i: (i, 0))
    hm = pl.BlockSpec((HEADS_PER_GROUP, tm, HEAD_DIM), lambda i: (0, i, 0))
    full = lambda a: pl.BlockSpec(a.shape, lambda i: (0,) * a.ndim)
    return pl.pallas_call(
        functools.partial(_post_kernel, n_exp=n_exp),
        out_shape=(jax.ShapeDtypeStruct((t, d), F32), jax.ShapeDtypeStruct((t, d + AFF_PAD), F32)),
        grid=(t // tm,),
        in_specs=[hm] * 6 + [
            row(BRANCH_WIDTH),
            pl.BlockSpec((tm, d), lambda i: (i, 0)),
            pl.BlockSpec((tm, d), lambda i: (i, 1)),
            row(d), full(wb), full(wo), full(lng), full(lnb), full(wr),
        ],
        out_specs=(row(d), row(d + AFF_PAD)),
        compiler_params=_cparams(("arbitrary",)),
        name="post_attn",
    )(o[0], o[1], o[2], lse[0], lse[1], lse[2], ob, zg, zg, x2, wb, wo, lng, lnb, wr)


def _topk_kernel(aff_ref, idx_ref, loc_ref, *, seq, cap, n_exp):
    b = pl.program_id(0)
    nb = seq // LANES
    blocks = [aff_ref[k * LANES:(k + 1) * LANES, :].T[:n_exp, :] for k in range(nb)]
    lane = lax.broadcasted_iota(jnp.int32, (n_exp, LANES), 1)
    i0 = lax.broadcasted_iota(jnp.int32, (LANES, LANES), 0)
    i1 = lax.broadcasted_iota(jnp.int32, (LANES, LANES), 1)
    incl = (i0 <= i1).astype(BF16)
    excl = (i0 < i1).astype(BF16)

    def count(pred):
        acc = jnp.zeros((n_exp, LANES), F32)
        for blk in blocks:
            acc = acc + pred(blk).astype(F32)
        return jnp.sum(acc, axis=1, keepdims=True)

    def search(i, t):
        cand = t | jnp.left_shift(jnp.int32(1), 30 - i)
        cf = pltpu.bitcast(cand, F32)
        return jnp.where(count(lambda blk: blk >= cf) >= cap, cand, t)

    thr = pltpu.bitcast(lax.fori_loop(0, 31, search, jnp.zeros((n_exp, LANES), jnp.int32)), F32)
    need = cap - count(lambda blk: blk > thr)

    def running(masks):
        local = [jnp.dot(m.astype(BF16), incl, preferred_element_type=F32) for m in masks]
        tot = jnp.zeros((n_exp, LANES), F32)
        for k, part in enumerate(local):
            tot = jnp.where(lane == k, part[:, LANES - 1:LANES], tot)
        return local, tot, jnp.dot(tot.astype(BF16), excl, preferred_element_type=F32)

    eq_local, _, eq_off = running([blk == thr for blk in blocks])
    sel = [(blk > thr) | ((blk == thr) & (eq_local[k] + eq_off[:, k:k + 1] <= need))
           for k, blk in enumerate(blocks)]
    sel_local, sel_tot, sel_off = running(sel)
    for k, part in enumerate(sel_local):
        loc_ref[k * n_exp:(k + 1) * n_exp, :] = part
    cum_end = sel_off + sel_tot

    slot = lax.broadcasted_iota(jnp.int32, (cap, 1), 0).astype(F32)
    lane_c = lax.broadcasted_iota(jnp.int32, (cap, LANES), 1)
    lane_f = lane_c.astype(F32)
    pad = jnp.zeros((LANES - nb, LANES), F32)
    res = jnp.zeros((cap, LANES), F32)
    for e in range(n_exp):
        blk_of = jnp.sum((cum_end[e:e + 1, :] <= slot).astype(F32), axis=1, keepdims=True)
        onehot = lane_f == blk_of
        before = jnp.sum(jnp.where(onehot, sel_off[e:e + 1, :], 0.0), axis=1, keepdims=True)
        table = jnp.concatenate([loc_ref[pl.ds(e, nb, stride=n_exp), :], pad], axis=0).astype(BF16)
        local = jnp.dot(onehot.astype(BF16), table, preferred_element_type=F32)
        pos = jnp.sum((local <= slot - before).astype(F32), axis=1, keepdims=True)
        res = jnp.where(lane_c == e, blk_of * LANES + pos, res)
    idx_ref[0] = res.T[:n_exp, :].astype(jnp.int32) + b * seq


def _topk(yext, batch, seq, d, n_exp, cap):
    assert n_exp % 8 == 0 and seq // LANES <= LANES and cap % LANES == 0
    return pl.pallas_call(
        functools.partial(_topk_kernel, seq=seq, cap=cap, n_exp=n_exp),
        out_shape=jax.ShapeDtypeStruct((batch, n_exp, cap), jnp.int32),
        grid=(batch,),
        in_specs=[pl.BlockSpec((seq, AFF_PAD), lambda b: (b, d // AFF_PAD))],
        out_specs=pl.BlockSpec((1, n_exp, cap), lambda b: (b, 0, 0)),
        scratch_shapes=[pltpu.VMEM((seq // LANES * n_exp, LANES), F32)],
        compiler_params=_cparams(("arbitrary",)),
        name="expert_topk",
    )(yext)


def _ffn_kernel(idx_ref, x1_hbm, yin_hbm, wg_ref, wu_ref, wd_ref, yext_hbm, xg_ref, xb_ref, yg_ref, acc_ref, sems,
                *, rows, d, n_split, nf):
    del yin_hbm
    e, hf, f = pl.program_id(0), pl.program_id(1), pl.program_id(2)
    n_groups = pl.num_programs(0) * n_split
    grp = e * n_split + hf
    base = grp * rows
    nxt = jnp.minimum(grp + 1, n_groups - 1) * rows
    prev = jnp.where(grp > 0, base - rows, rows)
    per = rows // nf
    yg_cur, yg_prev = yg_ref.at[grp % 2], yg_ref.at[(grp + 1) % 2]

    def x_copy(r, tok):
        return pltpu.make_async_copy(x1_hbm.at[pl.ds(tok, 1)], xg_ref.at[pl.ds(r, 1)], sems.at[0])

    def y_in_copy(buf, r, tok):
        return pltpu.make_async_copy(yext_hbm.at[pl.ds(tok, 1)], buf.at[pl.ds(r, 1)], sems.at[1])

    def y_out_copy(buf, r, tok):
        return pltpu.make_async_copy(buf.at[pl.ds(r, 1)], yext_hbm.at[pl.ds(tok, 1)], sems.at[2])

    def for_rows(fn, start):
        def body(r, c):
            fn(r, idx_ref[start + r])
            return c
        lax.fori_loop(0, rows, body, 0, unroll=8)

    @pl.when((f == 0) & (grp == 0))
    def _():
        for_rows(lambda r, tok: x_copy(r, tok).start(), base)
        for_rows(lambda r, tok: y_in_copy(yg_prev, r, tok).start(), prev)
        for_rows(lambda r, tok: y_in_copy(yg_prev, r, tok).wait(), prev)
        acc_ref[...] = jnp.zeros_like(acc_ref)

    @pl.when(f == 0)
    def _():
        for_rows(lambda r, tok: x_copy(r, tok).wait(), base)
        xb_ref[...] = xg_ref[...].astype(BF16)

    for r in range(per):
        row = f * per + r
        x_copy(row, idx_ref[nxt + row]).start()
        y_in_copy(yg_cur, row, idx_ref[base + row]).start()
        y_out_copy(yg_prev, row, idx_ref[prev + row]).start()

    wg, wu, wd = wg_ref[0].astype(BF16), wu_ref[0].astype(BF16), wd_ref[0].astype(BF16)
    sub = rows // 2
    for rs in (pl.ds(0, sub), pl.ds(sub, sub)):
        x = xb_ref[rs, :]
        g = jnp.dot(x, wg, preferred_element_type=F32)
        u = jnp.dot(x, wu, preferred_element_type=F32)
        hid = (g * jax.nn.sigmoid(g)) * u
        part = jnp.dot(hid.astype(BF16), wd, preferred_element_type=F32)
        acc_ref[rs, :] = jnp.where(f == 0, part, acc_ref[rs, :] + part)

    @pl.when(f == nf - 1)
    def _():
        for_rows(lambda r, tok: y_in_copy(yg_cur, r, tok).wait(), base)
        for_rows(lambda r, tok: y_out_copy(yg_prev, r, tok).wait(), prev)
        aff = yg_cur[:, d:]
        lane = lax.broadcasted_iota(jnp.int32, aff.shape, 1)
        gate = jnp.sum(jnp.where(lane == e, aff, 0.0), axis=-1, keepdims=True)
        yg_cur[:, :d] = yg_cur[:, :d] + acc_ref[...] * gate

    @pl.when((f == nf - 1) & (grp == n_groups - 1))
    def _():
        for_rows(lambda r, tok: x_copy(r, tok).wait(), base)
        for_rows(lambda r, tok: y_out_copy(yg_cur, r, tok).start(), base)
        for_rows(lambda r, tok: y_out_copy(yg_cur, r, tok).wait(), base)


def _expert_ffn(idx_flat, x1, yext, w_gate, w_up, w_down, cap, tf=256, n_split=2):
    t, d = x1.shape
    n_exp, _, ff = w_gate.shape
    tf = min(tf, ff)
    nf = ff // tf
    rows = idx_flat.shape[0] // (n_exp * n_split)
    assert rows % nf == 0 and n_split == 2 and rows % cap == 0
    grid_spec = pltpu.PrefetchScalarGridSpec(
        num_scalar_prefetch=1,
        grid=(n_exp, n_split, nf),
        in_specs=[
            pl.BlockSpec(memory_space=pl.ANY),
            pl.BlockSpec(memory_space=pl.ANY),
            pl.BlockSpec((1, d, tf), lambda e, h, f, idx: (e, 0, f)),
            pl.BlockSpec((1, d, tf), lambda e, h, f, idx: (e, 0, f)),
            pl.BlockSpec((1, tf, d), lambda e, h, f, idx: (e, f, 0)),
        ],
        out_specs=pl.BlockSpec(memory_space=pl.ANY),
        scratch_shapes=[
            pltpu.VMEM((rows, d), F32),
            pltpu.VMEM((rows, d), BF16),
            pltpu.VMEM((2, rows, d + AFF_PAD), F32),
            pltpu.VMEM((rows, d), F32),
            pltpu.SemaphoreType.DMA((3,)),
        ],
    )
    return pl.pallas_call(
        functools.partial(_ffn_kernel, rows=rows, d=d, n_split=n_split, nf=nf),
        out_shape=jax.ShapeDtypeStruct(yext.shape, yext.dtype),
        grid_spec=grid_spec,
        input_output_aliases={2: 0},
        compiler_params=_cparams(("arbitrary", "arbitrary", "arbitrary")),
        name="expert_ffn",
    )(idx_flat, x1, yext, w_gate, w_up, w_down)


def _ln2_kernel(y_ref, g_ref, b_ref, o_ref):
    o_ref[...] = _layer_norm(y_ref[...], g_ref[...], b_ref[...])


def _ln2(yext, d, g, b, tm=512):
    t = yext.shape[0]
    return pl.pallas_call(
        _ln2_kernel,
        out_shape=jax.ShapeDtypeStruct((t, d), F32),
        grid=(t // tm,),
        in_specs=[pl.BlockSpec((tm, d), lambda i: (i, 0)),
                  pl.BlockSpec((1, d), lambda i: (0, 0)), pl.BlockSpec((1, d), lambda i: (0, 0))],
        out_specs=pl.BlockSpec((tm, d), lambda i: (i, 0)),
        compiler_params=_cparams(("arbitrary",)),
        name="ln2",
    )(yext, g, b)


def kernel(x, w_in, lambda_q1, lambda_k1, lambda_q2, lambda_k2, diff_norm_w, w_branch, w_out, ln1_g, ln1_b,
           w_router, w_gate, w_up, w_down, ln2_g, ln2_b):
    batch, seq, d = x.shape
    assert w_in.shape[0] == 1, "one layer"
    n_exp = w_router.shape[-1]
    cap = EC_CAPACITY * seq // n_exp
    x2 = x.reshape(batch * seq, d)

    zn, zg, zd1, zd2 = _in_proj(x2, w_in[0], batch, seq)
    zn4 = zn.reshape(batch, 1, seq, zn.shape[1])
    o, lse = zip(_band_attention(zn4, (ZN_TILE["qa"], ZN_TILE["ka"], ZN_TILE["va"]), batch, seq, 0),
                 _band_attention(zd1, (0, 1, 2), batch, seq, 1),
                 _band_attention(zd2, (0, 1, 2), batch, seq, 2))
    ob = _diff_attention(zn, batch, seq, lambda_q1, lambda_k1, lambda_q2, lambda_k2, diff_norm_w)

    wr = jnp.pad(w_router[0], ((0, 0), (0, AFF_PAD - n_exp))).astype(BF16)
    x1, yext = _post(o, lse, ob, zg, x2, w_branch[0].astype(BF16), w_out[0].astype(BF16),
                     ln1_g, ln1_b, wr, n_exp)
    idx = _topk(yext, batch, seq, d, n_exp, cap)
    idx_flat = idx.transpose(1, 0, 2).reshape(-1)
    yext = _expert_ffn(idx_flat, x1, yext, w_gate[0], w_up[0], w_down[0], cap)
    return _ln2(yext, d, ln2_g, ln2_b).reshape(batch, seq, d)
```

```python
import functools
import math

import jax
import jax.numpy as jnp
from jax import lax
from jax.experimental import pallas as pl
from jax.experimental.pallas import tpu as pltpu

F32 = jnp.float32
BF16 = jnp.bfloat16

HEAD_DIM = 128
DIL_GROUPS = ((128, 1), (512, 4), (2048, 16))
HEADS_PER_GROUP = 4
N_GROUPS = len(DIL_GROUPS)
N_DIFF_HEADS = 4
BRANCH_WIDTH = HEADS_PER_GROUP * HEAD_DIM
ROPE_THETA = 500000.0
ROPE_FRACTION = 4
EC_CAPACITY = 2
LN_EPS = 1e-5
DIFF_NORM_EPS = 1e-5
NEG_INF = -1e30
LAMBDA_INIT = 0.8 - 0.6 * math.exp(-0.3 * 0)
ALPHA = 2.0 ** 0.25
DIFF_Q_SCALE = (HEAD_DIM // 2) ** -0.5 * math.log2(math.e)
BAND_Q_SCALE = HEAD_DIM ** -0.5 * math.log2(math.e)
LN2 = math.log(2.0)

N_DIL_TILES = 3 * N_GROUPS
QB_TILE, KB_TILE, VB_TILE = N_DIL_TILES, N_DIL_TILES + 1, N_DIL_TILES + 2
GATE_TILE = N_DIL_TILES + 3
LANES = 128
AFF_PAD = LANES
BAND_TQ = 128
VMEM_LIMIT = 56 * 1024 * 1024


def _cparams(sem):
    return pltpu.CompilerParams(dimension_semantics=sem, vmem_limit_bytes=VMEM_LIMIT)


def _rope_tables(seq, dh):
    rot = dh // ROPE_FRACTION
    half = rot // 2
    inv_freq = ROPE_THETA ** (-2.0 * jnp.arange(half, dtype=F32) / rot)
    ang = jnp.arange(seq).astype(F32)[:, None] * inv_freq[None, :]
    cos, sin = jnp.cos(ang), jnp.sin(ang)
    one = jnp.ones((seq, dh - rot), F32)
    zero = lambda n: jnp.zeros((seq, n), F32)
    c = jnp.concatenate([cos, cos, one], axis=-1)
    s_neg = jnp.concatenate([-sin, zero(dh - half)], axis=-1)
    s_pos = jnp.concatenate([zero(half), sin, zero(dh - rot)], axis=-1)
    tab = jnp.stack([c, s_neg, s_pos])
    return jnp.tile(tab, (1, 1, LANES // dh)), half


def _slabs(acc):
    return [acc[:, h * LANES:(h + 1) * LANES] for h in range(acc.shape[1] // LANES)]


def _in_proj_kernel(x_ref, w_ref, w2_ref, ra_ref, rb_ref, zn_ref, zg_ref, zd1_ref, zd2_ref, xb_ref, slab_ref,
                    *, half_a, half_b):
    j = pl.program_id(1)
    tm = xb_ref.shape[0]
    sub = tm // 2

    @pl.when(j == 0)
    def _():
        xb_ref[...] = x_ref[...].astype(BF16)

    is_gate = j >= GATE_TILE
    is_dil = j < N_DIL_TILES
    grp = j % N_GROUPS
    dil_rope = j < 2 * N_GROUPS
    is_b = jnp.logical_or(j == QB_TILE, j == KB_TILE)

    def project(rs):
        return _slabs(jnp.dot(xb_ref[rs, :], w_ref[...].astype(BF16), preferred_element_type=F32))

    @pl.when(is_gate)
    def _():
        w2 = jnp.concatenate([w_ref[...].astype(BF16), w2_ref[...].astype(BF16)], axis=1)
        zg_ref[...] = jnp.dot(xb_ref[...], w2, preferred_element_type=F32).astype(BF16)

    def natural(rope):
        for k in range(2):
            rs = pl.ds(k * sub, sub)
            zn_ref[rs, :] = jnp.concatenate([rope(a, rs) for a in project(rs)], axis=1).astype(BF16)

    def phase_major(rope, out_ref, r):
        rows = sub // r
        for k in range(2):
            rs = pl.ds(k * sub, sub)
            slabs = project(rs)
            for h, a in enumerate(slabs):
                slab_ref[h, rs, :] = rope(a, rs)
            for p in range(r):
                out_ref[0, p, pl.ds(k * rows, rows), :] = jnp.concatenate(
                    [slab_ref[h, pl.ds(k * sub + p, rows, stride=r), :] for h in range(len(slabs))],
                    axis=1).astype(BF16)

    def rope_with(tab_ref, half, scale=None):
        def rope(a, rs):
            out = (a * tab_ref[0, rs, :] + pltpu.roll(a, LANES - half, 1) * tab_ref[1, rs, :]
                   + pltpu.roll(a, half, 1) * tab_ref[2, rs, :])
            return out if scale is None else out * scale
        return rope

    no_rope = lambda a, rs: a

    rope_qk = rope_with(ra_ref, half_a, jnp.where(j < N_GROUPS, BAND_Q_SCALE, 1.0).astype(F32))

    @pl.when(is_dil & dil_rope & (grp == 0))
    def _():
        natural(rope_qk)

    for rope, cond in ((rope_qk, dil_rope), (no_rope, jnp.logical_not(dil_rope))):
        @pl.when(is_dil & cond & (grp == 1))
        def _(rope=rope):
            phase_major(rope, zd1_ref, DIL_GROUPS[1][1])

        @pl.when(is_dil & cond & (grp == 2))
        def _(rope=rope):
            phase_major(rope, zd2_ref, DIL_GROUPS[2][1])

    @pl.when(is_b)
    def _():
        natural(rope_with(rb_ref, half_b, jnp.where(j == QB_TILE, DIFF_Q_SCALE, 1.0).astype(F32)))

    @pl.when(jnp.logical_not(is_gate | is_b) & jnp.logical_not(is_dil & (dil_rope | (grp != 0))))
    def _():
        zn_ref[...] = jnp.dot(xb_ref[...], w_ref[...].astype(BF16), preferred_element_type=F32).astype(BF16)


ZN_TILE = dict(qa=0, ka=1, va=2, qb=3, kb=4, vb=5)


def _in_proj(x2, w, batch, seq, tm=1024):
    t, d = x2.shape
    n_tiles = w.shape[1] // BRANCH_WIDTH
    n_gate_tiles = n_tiles - GATE_TILE
    assert n_gate_tiles % 2 == 0
    tn = BRANCH_WIDTH
    tm = min(tm, seq)
    spt = seq // tm
    ra, half_a = _rope_tables(seq, HEAD_DIM)
    rb, half_b = _rope_tables(seq, HEAD_DIM // 2)
    r1, r2 = DIL_GROUPS[1][1], DIL_GROUPS[2][1]
    pair = lambda j: jnp.maximum(j - GATE_TILE, 0)

    def zn_col(j):
        jc = jnp.minimum(j, GATE_TILE - 1)
        return jnp.where(jc >= N_DIL_TILES, jc - N_DIL_TILES + ZN_TILE["qb"], ZN_TILE["qa"] + jc // N_GROUPS)

    def zd_col(g):
        return lambda j: jnp.clip((j - g) // N_GROUPS, 0, 2)

    return pl.pallas_call(
        functools.partial(_in_proj_kernel, half_a=half_a, half_b=half_b),
        out_shape=(
            jax.ShapeDtypeStruct((t, 6 * tn), BF16),
            jax.ShapeDtypeStruct((t, n_gate_tiles * tn), BF16),
            jax.ShapeDtypeStruct((batch, r1, seq // r1, 3 * tn), BF16),
            jax.ShapeDtypeStruct((batch, r2, seq // r2, 3 * tn), BF16),
        ),
        grid=(t // tm, GATE_TILE + n_gate_tiles // 2),
        in_specs=[
            pl.BlockSpec((tm, d), lambda i, j: (i, 0)),
            pl.BlockSpec((d, tn), lambda i, j: (0, jnp.where(j < GATE_TILE, j, GATE_TILE + 2 * pair(j)))),
            pl.BlockSpec((d, tn), lambda i, j: (0, GATE_TILE + 1 + 2 * pair(j))),
            pl.BlockSpec((3, tm, LANES), lambda i, j: (0, i % spt, 0), pipeline_mode=pl.Buffered(1)),
            pl.BlockSpec((3, tm, LANES), lambda i, j: (0, i % spt, 0), pipeline_mode=pl.Buffered(1)),
        ],
        out_specs=(
            pl.BlockSpec((tm, tn), lambda i, j: (i, zn_col(j))),
            pl.BlockSpec((tm, 2 * tn), lambda i, j: (i, pair(j))),
            pl.BlockSpec((1, r1, tm // r1, tn), lambda i, j: (i // spt, 0, i % spt, zd_col(1)(j))),
            pl.BlockSpec((1, r2, tm // r2, tn), lambda i, j: (i // spt, 0, i % spt, zd_col(2)(j))),
        ),
        scratch_shapes=[pltpu.VMEM((tm, d), BF16), pltpu.VMEM((tn // LANES, tm, LANES), F32)],
        compiler_params=_cparams(("arbitrary", "arbitrary")),
        name="in_proj",
    )(x2, w, w, ra, rb)


def _band_kernel(q_ref, k_ref, v_ref, o_ref, lse_ref, *, seq_l, tl, r, half):
    li = pl.program_id(1)
    win = BAND_TQ + 2 * half
    row = lax.broadcasted_iota(jnp.int32, (BAND_TQ, win), 0)
    col = lax.broadcasted_iota(jnp.int32, (BAND_TQ, win), 1)
    tiles = tl // BAND_TQ

    def body(it, carry):
        p = it // tiles
        t = it % tiles
        r0 = pl.multiple_of(t * BAND_TQ, BAND_TQ)
        q0 = li * tl + r0
        ws = pl.multiple_of(jnp.clip(q0 - half, 0, seq_l - win), half)
        valid = jnp.abs((q0 + row) - (ws + col)) <= half
        for h in range(HEADS_PER_GROUP):
            hs = slice(h * HEAD_DIM, (h + 1) * HEAD_DIM)
            q = q_ref[0, p, pl.ds(r0, BAND_TQ), hs]
            k = k_ref[0, p, pl.ds(ws, win), hs]
            v = v_ref[0, p, pl.ds(ws, win), hs]
            s = lax.dot_general(q, k, (((1,), (1,)), ((), ())), preferred_element_type=F32)
            s = jnp.where(valid, s, NEG_INF)
            m = jnp.max(s, axis=-1, keepdims=True)
            pr = jnp.exp2(s - m)
            den = jnp.sum(pr, axis=-1, keepdims=True)
            acc = jnp.dot(pr.astype(BF16), v, preferred_element_type=F32)
            dst = pl.ds(r0 * r + p, BAND_TQ, stride=r) if r > 1 else pl.ds(r0, BAND_TQ)
            o_ref[h, dst, :] = acc / den
            lse_ref[h, dst, :] = jnp.broadcast_to(m * LN2 + jnp.log(den), (BAND_TQ, HEAD_DIM))
        return carry

    lax.fori_loop(0, r * tiles, body, 0, unroll=4)


def _band_attention(zsrc, tiles_qkv, batch, seq, g):
    window, r = DIL_GROUPS[g]
    half = window // (2 * r)
    seq_l = seq // r
    tl = max(BAND_TQ, 1024 // r)
    tq_, tk_, tv_ = tiles_qkv
    t = batch * seq
    nl = seq_l // tl
    out_sd = jax.ShapeDtypeStruct((HEADS_PER_GROUP, t, HEAD_DIM), F32)
    o_spec = pl.BlockSpec((HEADS_PER_GROUP, tl * r, HEAD_DIM), lambda b, li: (0, b * nl + li, 0))
    return pl.pallas_call(
        functools.partial(_band_kernel, seq_l=seq_l, tl=tl, r=r, half=half),
        out_shape=(out_sd, out_sd),
        grid=(batch, nl),
        in_specs=[
            pl.BlockSpec((1, r, tl, BRANCH_WIDTH), lambda b, li: (b, 0, li, tq_)),
            pl.BlockSpec((1, r, seq_l, BRANCH_WIDTH), lambda b, li: (b, 0, 0, tk_)),
            pl.BlockSpec((1, r, seq_l, BRANCH_WIDTH), lambda b, li: (b, 0, 0, tv_)),
        ],
        out_specs=(o_spec, o_spec),
        compiler_params=_cparams(("arbitrary", "arbitrary")),
        name=f"band_attn_g{g}",
    )(zsrc, zsrc, zsrc)


def _diff_kernel(lq1_ref, lk1_ref, lq2_ref, lk2_ref, nw_ref, q_ref, k_ref, v_ref, o_ref, s_ref, *, seq, tk):
    lam = (jnp.exp(jnp.sum(lq1_ref[...] * lk1_ref[...])) - jnp.exp(jnp.sum(lq2_ref[...] * lk2_ref[...]))
           + LAMBDA_INIT)
    dc = HEAD_DIM // 2
    q = q_ref[0]
    tq = q.shape[0]
    lane = lax.broadcasted_iota(jnp.int32, (tq, HEAD_DIM), 1)
    qc = [jnp.where(lane < dc, q, jnp.zeros_like(q)), jnp.where(lane >= dc, q, jnp.zeros_like(q))]
    slabs = tk // LANES

    def scores(j, mx):
        kj = k_ref[0, pl.ds(pl.multiple_of(j * tk, tk), tk), :]
        out = []
        for c in range(2):
            s = lax.dot_general(qc[c], kj, (((1,), (1,)), ((), ())), preferred_element_type=F32)
            s_ref[c, j] = s
            m = mx[c]
            for t in range(slabs):
                m = jnp.maximum(m, s[:, t * LANES:(t + 1) * LANES])
            out.append(m)
        return tuple(out)

    neg = jnp.full((tq, LANES), -jnp.inf, F32)
    mx = lax.fori_loop(0, seq // tk, scores, (neg, neg), unroll=2)
    row_max = [jnp.broadcast_to(jnp.max(m, axis=-1, keepdims=True), (tq, LANES)) for m in mx]

    ones = jnp.ones((tk, LANES), BF16)

    def weigh(j, accs):
        vj = jnp.concatenate([v_ref[0, pl.ds(pl.multiple_of(j * tk, tk), tk), :], ones], axis=1)
        out = []
        for c in range(2):
            p = [jnp.exp2(s_ref[c, j, :, t * LANES:(t + 1) * LANES] - row_max[c]).astype(BF16)
                 for t in range(slabs)]
            out.append(accs[c] + jnp.dot(jnp.concatenate(p, axis=1), vj, preferred_element_type=F32))
        return tuple(out)

    zero = jnp.zeros((tq, 2 * LANES), F32)
    acc0, acc1 = lax.fori_loop(0, seq // tk, weigh, (zero, zero), unroll=2)
    o = (acc0[:, :HEAD_DIM] / acc0[:, HEAD_DIM:]) - lam * (acc1[:, :HEAD_DIM] / acc1[:, HEAD_DIM:])
    o = o * lax.rsqrt(jnp.mean(o * o, axis=-1, keepdims=True) + DIFF_NORM_EPS)
    o_ref[0] = o * nw_ref[...] * (1.0 - LAMBDA_INIT)


def _diff_attention(zn, batch, seq, lq1, lk1, lq2, lk2, norm_w, tq=1024, tk=2048):
    zt = ZN_TILE
    zv = zn.reshape(batch, seq, zn.shape[1])
    hpt = BRANCH_WIDTH // HEAD_DIM
    vec = lambda n: pl.BlockSpec((1, n), lambda b, h, qi: (0, 0))
    return pl.pallas_call(
        functools.partial(_diff_kernel, seq=seq, tk=tk),
        scratch_shapes=[pltpu.VMEM((2, seq // tk, tq, tk), F32)],
        out_shape=jax.ShapeDtypeStruct((batch, seq, BRANCH_WIDTH), F32),
        grid=(batch, N_DIFF_HEADS, seq // tq),
        in_specs=[
            vec(HEAD_DIM // 2), vec(HEAD_DIM // 2), vec(HEAD_DIM // 2), vec(HEAD_DIM // 2), vec(HEAD_DIM),
            pl.BlockSpec((1, tq, HEAD_DIM), lambda b, h, qi: (b, qi, zt["qb"] * hpt + h)),
            pl.BlockSpec((1, seq, HEAD_DIM), lambda b, h, qi: (b, 0, zt["kb"] * hpt + h)),
            pl.BlockSpec((1, seq, HEAD_DIM), lambda b, h, qi: (b, 0, zt["vb"] * hpt + h)),
        ],
        out_specs=pl.BlockSpec((1, tq, HEAD_DIM), lambda b, h, qi: (b, qi, h)),
        compiler_params=_cparams(("arbitrary", "arbitrary", "arbitrary")),
        name="diff_attn",
    )(lq1, lk1, lq2, lk2, norm_w, zv, zv, zv).reshape(batch * seq, BRANCH_WIDTH)


def _layer_norm(h, g, b):
    mu = jnp.mean(h, axis=-1, keepdims=True)
    hc = h - mu
    var = jnp.mean(hc * hc, axis=-1, keepdims=True)
    return hc * lax.rsqrt(var + LN_EPS) * g + b


def _post_kernel(o0_ref, o1_ref, o2_ref, l0_ref, l1_ref, l2_ref, ob_ref, g0_ref, g1_ref, x_ref,
                 wb_ref, wo_ref, lng_ref, lnb_ref, wr_ref, x1_ref, yext_ref, *, n_exp):
    d = x_ref.shape[1]
    heads = lambda ref: jnp.concatenate([ref[h] for h in range(HEADS_PER_GROUP)], axis=1)
    l0, l1, l2 = heads(l0_ref), heads(l1_ref), heads(l2_ref)
    lm = jnp.maximum(jnp.maximum(l0, l1), l2)
    e0, e1, e2 = jnp.exp(l0 - lm), jnp.exp(l1 - lm), jnp.exp(l2 - lm)
    den = e0 + e1 + e2
    oa = heads(o0_ref) * (e0 / den) + heads(o1_ref) * (e1 / den) + heads(o2_ref) * (e2 / den)
    bd0 = jnp.dot(oa.astype(BF16), wb_ref[0], preferred_element_type=F32)
    bd1 = jnp.dot(ob_ref[...].astype(BF16), wb_ref[1], preferred_element_type=F32)
    merged = jax.nn.sigmoid(g0_ref[...].astype(F32)) * bd0 + jax.nn.sigmoid(g1_ref[...].astype(F32)) * bd1
    mix = jnp.dot(merged.astype(BF16), wo_ref[...], preferred_element_type=F32)
    x1 = _layer_norm(ALPHA * x_ref[...] + mix, lng_ref[...], lnb_ref[...])
    logits = jnp.dot(x1.astype(BF16), wr_ref[...], preferred_element_type=F32)
    lane = lax.broadcasted_iota(jnp.int32, logits.shape, 1)
    logits = jnp.where(lane < n_exp, logits, -jnp.inf)
    ex = jnp.exp(logits - jnp.max(logits, axis=-1, keepdims=True))
    aff = ex / jnp.sum(ex, axis=-1, keepdims=True)
    x1_ref[...] = x1
    yext_ref[:, :d] = ALPHA * x1
    yext_ref[:, d:] = aff


def _post(o, lse, ob, zg, x2, wb, wo, lng, lnb, wr, n_exp, tm=256):
    t, d = x2.shape
    row = lambda w: pl.BlockSpec((tm, w), lambda i: (i, 0))
    hm = pl.BlockSpec((HEADS_PER_GROUP, tm, HEAD_DIM), lambda i: (0, i, 0))
    full = lambda a: pl.BlockSpec(a.shape, lambda i: (0,) * a.ndim)
    return pl.pallas_call(
        functools.partial(_post_kernel, n_exp=n_exp),
        out_shape=(jax.ShapeDtypeStruct((t, d), F32), jax.ShapeDtypeStruct((t, d + AFF_PAD), F32)),
        grid=(t // tm,),
        in_specs=[hm] * 6 + [
            row(BRANCH_WIDTH),
            pl.BlockSpec((tm, d), lambda i: (i, 0)),
            pl.BlockSpec((tm, d), lambda i: (i, 1)),
            row(d), full(wb), full(wo), full(lng), full(lnb), full(wr),
        ],
        out_specs=(row(d), row(d + AFF_PAD)),
        compiler_params=_cparams(("arbitrary",)),
        name="post_attn",
    )(o[0], o[1], o[2], lse[0], lse[1], lse[2], ob, zg, zg, x2, wb, wo, lng, lnb, wr)


def _topk_kernel(aff_ref, idx_ref, loc_ref, *, seq, cap, n_exp):
    b = pl.program_id(0)
    nb = seq // LANES
    blocks = [aff_ref[k * LANES:(k + 1) * LANES, :].T[:n_exp, :] for k in range(nb)]
    lane = lax.broadcasted_iota(jnp.int32, (n_exp, LANES), 1)
    i0 = lax.broadcasted_iota(jnp.int32, (LANES, LANES), 0)
    i1 = lax.broadcasted_iota(jnp.int32, (LANES, LANES), 1)
    incl = (i0 <= i1).astype(BF16)
    excl = (i0 < i1).astype(BF16)

    def count(pred):
        acc = jnp.zeros((n_exp, LANES), F32)
        for blk in blocks:
            acc = acc + pred(blk).astype(F32)
        return jnp.sum(acc, axis=1, keepdims=True)

    def search(i, t):
        cand = t | jnp.left_shift(jnp.int32(1), 30 - i)
        cf = pltpu.bitcast(cand, F32)
        return jnp.where(count(lambda blk: blk >= cf) >= cap, cand, t)

    thr = pltpu.bitcast(lax.fori_loop(0, 31, search, jnp.zeros((n_exp, LANES), jnp.int32)), F32)
    need = cap - count(lambda blk: blk > thr)

    def running(masks):
        local = [jnp.dot(m.astype(BF16), incl, preferred_element_type=F32) for m in masks]
        tot = jnp.zeros((n_exp, LANES), F32)
        for k, part in enumerate(local):
            tot = jnp.where(lane == k, part[:, LANES - 1:LANES], tot)
        return local, tot, jnp.dot(tot.astype(BF16), excl, preferred_element_type=F32)

    eq_local, _, eq_off = running([blk == thr for blk in blocks])
    sel = [(blk > thr) | ((blk == thr) & (eq_local[k] + eq_off[:, k:k + 1] <= need))
           for k, blk in enumerate(blocks)]
    sel_local, sel_tot, sel_off = running(sel)
    for k, part in enumerate(sel_local):
        loc_ref[k * n_exp:(k + 1) * n_exp, :] = part
    cum_end = sel_off + sel_tot

    slot = lax.broadcasted_iota(jnp.int32, (cap, 1), 0).astype(F32)
    lane_c = lax.broadcasted_iota(jnp.int32, (cap, LANES), 1)
    lane_f = lane_c.astype(F32)
    pad = jnp.zeros((LANES - nb, LANES), F32)
    res = jnp.zeros((cap, LANES), F32)
    for e in range(n_exp):
        blk_of = jnp.sum((cum_end[e:e + 1, :] <= slot).astype(F32), axis=1, keepdims=True)
        onehot = lane_f == blk_of
        before = jnp.sum(jnp.where(onehot, sel_off[e:e + 1, :], 0.0), axis=1, keepdims=True)
        table = jnp.concatenate([loc_ref[pl.ds(e, nb, stride=n_exp), :], pad], axis=0).astype(BF16)
        local = jnp.dot(onehot.astype(BF16), table, preferred_element_type=F32)
        pos = jnp.sum((local <= slot - before).astype(F32), axis=1, keepdims=True)
        res = jnp.where(lane_c == e, blk_of * LANES + pos, res)
    idx_ref[0] = res.T[:n_exp, :].astype(jnp.int32) + b * seq


def _topk(yext, batch, seq, d, n_exp, cap):
    assert n_exp % 8 == 0 and seq // LANES <= LANES and cap % LANES == 0
    return pl.pallas_call(
        functools.partial(_topk_kernel, seq=seq, cap=cap, n_exp=n_exp),
        out_shape=jax.ShapeDtypeStruct((batch, n_exp, cap), jnp.int32),
        grid=(batch,),
        in_specs=[pl.BlockSpec((seq, AFF_PAD), lambda b: (b, d // AFF_PAD))],
        out_specs=pl.BlockSpec((1, n_exp, cap), lambda b: (b, 0, 0)),
        scratch_shapes=[pltpu.VMEM((seq // LANES * n_exp, LANES), F32)],
        compiler_params=_cparams(("arbitrary",)),
        name="expert_topk",
    )(yext)


def _ffn_kernel(idx_ref, x1_hbm, yin_hbm, wg_ref, wu_ref, wd_ref, yext_hbm, xg_ref, xb_ref, yg_ref, acc_ref, sems,
                *, rows, d, n_split, nf):
    del yin_hbm
    e, hf, f = pl.program_id(0), pl.program_id(1), pl.program_id(2)
    n_groups = pl.num_programs(0) * n_split
    grp = e * n_split + hf
    base = grp * rows
    nxt = jnp.minimum(grp + 1, n_groups - 1) * rows
    prev = jnp.where(grp > 0, base - rows, rows)
    per = rows // nf
    yg_cur, yg_prev = yg_ref.at[grp % 2], yg_ref.at[(grp + 1) % 2]

    def x_copy(r, tok):
        return pltpu.make_async_copy(x1_hbm.at[pl.ds(tok, 1)], xg_ref.at[pl.ds(r, 1)], sems.at[0])

    def y_in_copy(buf, r, tok):
        return pltpu.make_async_copy(yext_hbm.at[pl.ds(tok, 1)], buf.at[pl.ds(r, 1)], sems.at[1])

    def y_out_copy(buf, r, tok):
        return pltpu.make_async_copy(buf.at[pl.ds(r, 1)], yext_hbm.at[pl.ds(tok, 1)], sems.at[2])

    def for_rows(fn, start):
        def body(r, c):
            fn(r, idx_ref[start + r])
            return c
        lax.fori_loop(0, rows, body, 0, unroll=8)

    @pl.when((f == 0) & (grp == 0))
    def _():
        for_rows(lambda r, tok: x_copy(r, tok).start(), base)
        for_rows(lambda r, tok: y_in_copy(yg_prev, r, tok).start(), prev)
        for_rows(lambda r, tok: y_in_copy(yg_prev, r, tok).wait(), prev)
        acc_ref[...] = jnp.zeros_like(acc_ref)

    @pl.when(f == 0)
    def _():
        for_rows(lambda r, tok: x_copy(r, tok).wait(), base)
        xb_ref[...] = xg_ref[...].astype(BF16)

    for r in range(per):
        row = f * per + r
        x_copy(row, idx_ref[nxt + row]).start(priority=1)
        y_in_copy(yg_cur, row, idx_ref[base + row]).start(priority=1)
        y_out_copy(yg_prev, row, idx_ref[prev + row]).start(priority=1)

    wg, wu, wd = wg_ref[0].astype(BF16), wu_ref[0].astype(BF16), wd_ref[0].astype(BF16)
    sub = rows // 2
    for rs in (pl.ds(0, sub), pl.ds(sub, sub)):
        x = xb_ref[rs, :]
        g = jnp.dot(x, wg, preferred_element_type=F32)
        u = jnp.dot(x, wu, preferred_element_type=F32)
        hid = (g * jax.nn.sigmoid(g)) * u
        part = jnp.dot(hid.astype(BF16), wd, preferred_element_type=F32)
        acc_ref[rs, :] = jnp.where(f == 0, part, acc_ref[rs, :] + part)

    @pl.when(f == nf - 1)
    def _():
        for_rows(lambda r, tok: y_in_copy(yg_cur, r, tok).wait(), base)
        for_rows(lambda r, tok: y_out_copy(yg_prev, r, tok).wait(), prev)
        aff = yg_cur[:, d:]
        lane = lax.broadcasted_iota(jnp.int32, aff.shape, 1)
        gate = jnp.sum(jnp.where(lane == e, aff, 0.0), axis=-1, keepdims=True)
        yg_cur[:, :d] = yg_cur[:, :d] + acc_ref[...] * gate

    @pl.when((f == nf - 1) & (grp == n_groups - 1))
    def _():
        for_rows(lambda r, tok: x_copy(r, tok).wait(), base)
        for_rows(lambda r, tok: y_out_copy(yg_cur, r, tok).start(), base)
        for_rows(lambda r, tok: y_out_copy(yg_cur, r, tok).wait(), base)


def _expert_ffn(idx_flat, x1, yext, w_gate, w_up, w_down, cap, tf=256, n_split=2):
    t, d = x1.shape
    n_exp, _, ff = w_gate.shape
    tf = min(tf, ff)
    nf = ff // tf
    rows = idx_flat.shape[0] // (n_exp * n_split)
    assert rows % nf == 0 and n_split == 2 and rows % cap == 0
    grid_spec = pltpu.PrefetchScalarGridSpec(
        num_scalar_prefetch=1,
        grid=(n_exp, n_split, nf),
        in_specs=[
            pl.BlockSpec(memory_space=pl.ANY),
            pl.BlockSpec(memory_space=pl.ANY),
            pl.BlockSpec((1, d, tf), lambda e, h, f, idx: (e, 0, f)),
            pl.BlockSpec((1, d, tf), lambda e, h, f, idx: (e, 0, f)),
            pl.BlockSpec((1, tf, d), lambda e, h, f, idx: (e, f, 0)),
        ],
        out_specs=pl.BlockSpec(memory_space=pl.ANY),
        scratch_shapes=[
            pltpu.VMEM((rows, d), F32),
            pltpu.VMEM((rows, d), BF16),
            pltpu.VMEM((2, rows, d + AFF_PAD), F32),
            pltpu.VMEM((rows, d), F32),
            pltpu.SemaphoreType.DMA((3,)),
        ],
    )
    return pl.pallas_call(
        functools.partial(_ffn_kernel, rows=rows, d=d, n_split=n_split, nf=nf),
        out_shape=jax.ShapeDtypeStruct(yext.shape, yext.dtype),
        grid_spec=grid_spec,
        input_output_aliases={2: 0},
        compiler_params=_cparams(("arbitrary", "arbitrary", "arbitrary")),
        name="expert_ffn",
    )(idx_flat, x1, yext, w_gate, w_up, w_down)


def _ln2_kernel(y_ref, g_ref, b_ref, o_ref):
    o_ref[...] = _layer_norm(y_ref[...], g_ref[...], b_ref[...])


def _ln2(yext, d, g, b, tm=512):
    t = yext.shape[0]
    return pl.pallas_call(
        _ln2_kernel,
        out_shape=jax.ShapeDtypeStruct((t, d), F32),
        grid=(t // tm,),
        in_specs=[pl.BlockSpec((tm, d), lambda i: (i, 0)),
                  pl.BlockSpec((1, d), lambda i: (0, 0)), pl.BlockSpec((1, d), lambda i: (0, 0))],
        out_specs=pl.BlockSpec((tm, d), lambda i: (i, 0)),
        compiler_params=_cparams(("arbitrary",)),
        name="ln2",
    )(yext, g, b)


def kernel(x, w_in, lambda_q1, lambda_k1, lambda_q2, lambda_k2, diff_norm_w, w_branch, w_out, ln1_g, ln1_b,
           w_router, w_gate, w_up, w_down, ln2_g, ln2_b):
    batch, seq, d = x.shape
    assert w_in.shape[0] == 1, "one layer"
    n_exp = w_router.shape[-1]
    cap = EC_CAPACITY * seq // n_exp
    x2 = x.reshape(batch * seq, d)

    zn, zg, zd1, zd2 = _in_proj(x2, w_in[0], batch, seq)
    zn4 = zn.reshape(batch, 1, seq, zn.shape[1])
    o, lse = zip(_band_attention(zn4, (ZN_TILE["qa"], ZN_TILE["ka"], ZN_TILE["va"]), batch, seq, 0),
                 _band_attention(zd1, (0, 1, 2), batch, seq, 1),
                 _band_attention(zd2, (0, 1, 2), batch, seq, 2))
    ob = _diff_attention(zn, batch, seq, lambda_q1, lambda_k1, lambda_q2, lambda_k2, diff_norm_w)

    wr = jnp.pad(w_router[0], ((0, 0), (0, AFF_PAD - n_exp))).astype(BF16)
    x1, yext = _post(o, lse, ob, zg, x2, w_branch[0].astype(BF16), w_out[0].astype(BF16),
                     ln1_g, ln1_b, wr, n_exp)
    idx = _topk(yext, batch, seq, d, n_exp, cap)
    idx_flat = idx.transpose(1, 0, 2).reshape(-1)
    yext = _expert_ffn(idx_flat, x1, yext, w_gate[0], w_up[0], w_down[0], cap)
    return _ln2(yext, d, ln2_g, ln2_b).reshape(batch, seq, d)
```
